```python
import math
import jax, jax.numpy as jnp
from jax import lax
import numpy as np

D_MODEL = 2048
BATCH = 1
SEQ = 16384
DEPTH = 1
DEC_BATCH = 8
DEC_SEQ = 4096
PAST_LEN = 128

HEAD_DIM = 128
A_HEADS = 8
A_KV_HEADS = 2
A_GROUP = A_HEADS // A_KV_HEADS
WINDOW = 128
BLOCK = 128
ROPE_THETA = 500000.0
ROPE_DIM = HEAD_DIM // 4
B_HEADS = 4
GRID_W = 64
NA_ROWS = 8
NA_COLS = 16
M_HEADS = 4
N_MEM = 256
A_Q_W = A_HEADS * HEAD_DIM
A_KV_W = A_KV_HEADS * HEAD_DIM
B_W = B_HEADS * HEAD_DIM
M_W = M_HEADS * HEAD_DIM
N_BRANCH = 3
IN_SIZES = (A_Q_W, A_KV_W, A_KV_W, B_W, B_W, B_W, M_W, N_BRANCH * D_MODEL)
IN_W = A_Q_W + 2 * A_KV_W + 3 * B_W + M_W + N_BRANCH * D_MODEL
PEER_HEADS = 8
PEER_QDIM = 256
PEER_HALF = PEER_QDIM // 2
N_KEYS = 128
N_EXPERTS = N_KEYS * N_KEYS
PEER_TOPK = 16
PEER_BLOCK = 128
DEEPNORM_ALPHA = (2.0 * DEPTH) ** 0.25
DEEPNORM_BETA = (8.0 * DEPTH) ** -0.25
LN_EPS = 1e-5
NEG_INF = -1e30

kernel_name = "hybrid_gated_window_natten_peer_encoder"


def layer_norm(x, g, b):
    xf = x.astype(jnp.float32)
    mu = jnp.mean(xf, axis=-1, keepdims=True)
    xc = xf - mu
    var = jnp.mean(xc * xc, axis=-1, keepdims=True)
    y = xc * lax.rsqrt(var + LN_EPS) * g.astype(jnp.float32) + b.astype(jnp.float32)
    return y.astype(x.dtype)


def rope_partial(x, pos):
    half = ROPE_DIM // 2
    inv_freq = ROPE_THETA ** (-jnp.arange(half, dtype=jnp.float32) * 2.0 / ROPE_DIM)
    ang = pos.astype(jnp.float32)[:, None] * inv_freq[None, :]
    cos = jnp.cos(ang)[None, :, None, :]
    sin = jnp.sin(ang)[None, :, None, :]
    xf = x.astype(jnp.float32)
    x1 = xf[..., :half]
    x2 = xf[..., half:ROPE_DIM]
    out = jnp.concatenate([x1 * cos - x2 * sin, x2 * cos + x1 * sin, xf[..., ROPE_DIM:]], axis=-1)
    return out.astype(x.dtype)


def windowed_gqa(q, k, v, sink):
    Bsz, T = q.shape[0], q.shape[1]
    nb = T // BLOCK
    scale = HEAD_DIM ** -0.5
    qb = q.reshape(Bsz, nb, BLOCK, A_KV_HEADS, A_GROUP, HEAD_DIM)
    pad = ((0, 0), (BLOCK, BLOCK), (0, 0), (0, 0))
    kp = jnp.pad(k, pad).reshape(Bsz, nb + 2, BLOCK, A_KV_HEADS, HEAD_DIM)
    vp = jnp.pad(v, pad).reshape(Bsz, nb + 2, BLOCK, A_KV_HEADS, HEAD_DIM)
    kw = jnp.concatenate([kp[:, :-2], kp[:, 1:-1], kp[:, 2:]], axis=2)
    vw = jnp.concatenate([vp[:, :-2], vp[:, 1:-1], vp[:, 2:]], axis=2)
    s = jnp.einsum('bnqhgd,bnkhd->bnhgqk', qb, kw,
                   preferred_element_type=jnp.float32) * scale
    qi = jnp.arange(BLOCK)[:, None]
    kj = jnp.arange(3 * BLOCK)[None, :]
    in_band = jnp.abs(kj - BLOCK - qi) <= WINDOW
    kpos = (jnp.arange(nb)[:, None] - 1) * BLOCK + jnp.arange(3 * BLOCK)[None, :]
    in_seq = (kpos >= 0) & (kpos < T)
    mask = in_band[None] & in_seq[:, None, :]
    s = jnp.where(mask[None, :, None, None], s, NEG_INF)
    sink_l = sink.astype(jnp.float32).reshape(A_KV_HEADS, A_GROUP)[None, None, :, :, None]
    lse = jnp.logaddexp(jax.nn.logsumexp(s, axis=-1), sink_l)
    p = jnp.exp(s - lse[..., None])
    o = jnp.einsum('bnhgqk,bnkhd->bnqhgd', p.astype(v.dtype), vw)
    return o.reshape(Bsz, T, A_Q_W)


def neighbourhood_attn(q, k, v, rpb):
    Bsz, T = q.shape[0], q.shape[1]
    rows = T // GRID_W
    wr = min(NA_ROWS, rows)
    scale = HEAD_DIM ** -0.5
    r = jnp.arange(rows)
    rs = jnp.clip(r - wr // 2, 0, rows - wr)
    key_rows = rs[:, None] + jnp.arange(wr)[None, :]
    c = jnp.arange(GRID_W)
    cs = jnp.clip(c - NA_COLS // 2, 0, GRID_W - NA_COLS)
    col_ok = (c[None, :] >= cs[:, None]) & (c[None, :] < cs[:, None] + NA_COLS)
    dr = key_rows - r[:, None]
    dc = jnp.clip(c[None, :] - c[:, None], -(NA_COLS - 1), NA_COLS - 1)
    bias = rpb[:, (dr + NA_ROWS - 1)[:, None, :, None], (dc + NA_COLS - 1)[None, :, None, :]]
    bias = jnp.where(col_ok[None, None, :, None, :], bias.astype(jnp.float32), NEG_INF)
    bias = bias.reshape(B_HEADS, rows, GRID_W, wr * GRID_W).transpose(1, 0, 2, 3)
    qg = q.reshape(Bsz, rows, GRID_W, B_HEADS, HEAD_DIM)
    kg = k.reshape(Bsz, rows, GRID_W, B_HEADS, HEAD_DIM)[:, key_rows].reshape(
        Bsz, rows, wr * GRID_W, B_HEADS, HEAD_DIM)
    vg = v.reshape(Bsz, rows, GRID_W, B_HEADS, HEAD_DIM)[:, key_rows].reshape(
        Bsz, rows, wr * GRID_W, B_HEADS, HEAD_DIM)
    s = jnp.einsum('brqhd,brkhd->brhqk', qg, kg,
                   preferred_element_type=jnp.float32) * scale + bias[None]
    p = jax.nn.softmax(s, axis=-1)
    o = jnp.einsum('brhqk,brkhd->brqhd', p.astype(v.dtype), vg)
    return o.reshape(Bsz, T, B_W)


def memory_attn(q, mem, w_mem_kv):
    Bsz, T = q.shape[0], q.shape[1]
    kv = jnp.einsum('bmd,de->bme', mem, w_mem_kv)
    km = kv[..., :M_W].reshape(Bsz, -1, M_HEADS, HEAD_DIM)
    vm = kv[..., M_W:].reshape(Bsz, -1, M_HEADS, HEAD_DIM)
    s = jnp.einsum('bthd,bmhd->bhtm', q, km, preferred_element_type=jnp.float32) * HEAD_DIM ** -0.5
    p = jax.nn.softmax(s, axis=-1)
    o = jnp.einsum('bhtm,bmhd->bthd', p.astype(vm.dtype), vm)
    return o.reshape(Bsz, T, M_W)


def peer_ffn(x, w_peer_q, peer_keys1, peer_keys2, peer_u, peer_v):
    Bsz, T, D = x.shape
    q = jnp.einsum('btd,de->bte', x, w_peer_q).reshape(Bsz, T, PEER_HEADS, 2, PEER_HALF)
    s1 = jnp.einsum('bthc,nc->bthn', q[..., 0, :], peer_keys1, preferred_element_type=jnp.float32)
    s2 = jnp.einsum('bthc,nc->bthn', q[..., 1, :], peer_keys2, preferred_element_type=jnp.float32)
    t1, i1 = lax.top_k(s1, PEER_TOPK)
    t2, i2 = lax.top_k(s2, PEER_TOPK)
    cand = (t1[..., :, None] + t2[..., None, :]).reshape(Bsz, T, PEER_HEADS, PEER_TOPK * PEER_TOPK)
    cand_idx = (i1[..., :, None] * N_KEYS + i2[..., None, :]).reshape(
        Bsz, T, PEER_HEADS, PEER_TOPK * PEER_TOPK)
    top, pos = lax.top_k(cand, PEER_TOPK)
    idx = jnp.take_along_axis(cand_idx, pos, axis=-1)
    g = jax.nn.softmax(top, axis=-1)
    n_blk = (Bsz * T) // PEER_BLOCK
    xs = x.reshape(n_blk, PEER_BLOCK, D)
    idxs = idx.reshape(n_blk, PEER_BLOCK, PEER_HEADS, PEER_TOPK)
    gs = g.reshape(n_blk, PEER_BLOCK, PEER_HEADS, PEER_TOPK)

    def expert_block(args):
        xb, ib, gb = args
        ub = peer_u[ib]
        a = jnp.einsum('td,thkd->thk', xb, ub, preferred_element_type=jnp.float32)
        w = gb * jax.nn.gelu(a, approximate=False)
        vb = peer_v[ib]
        return jnp.einsum('thk,thkd->td', w.astype(vb.dtype), vb)

    out = lax.map(expert_block, (xs, idxs, gs))
    return out.reshape(Bsz, T, D)


def encoder_layer(x, mem, w_in, b_gate, a_sink, na_rpb, w_mem_kv, w_proj_a, w_proj_b, w_proj_m,
                  w_out, ln1_g, ln1_b, w_peer_q, peer_keys1, peer_keys2, peer_u, peer_v,
                  ln2_g, ln2_b):
    Bsz, T, _ = x.shape
    proj = jnp.einsum('btd,de->bte', x, w_in)
    pieces = []
    off = 0
    for width in IN_SIZES:
        pieces.append(proj[..., off:off + width])
        off += width
    q_a, k_a, v_a, q_b, k_b, v_b, q_m, gate_logits = pieces
    pos = jnp.arange(T)
    q_a = rope_partial(q_a.reshape(Bsz, T, A_HEADS, HEAD_DIM), pos)
    k_a = rope_partial(k_a.reshape(Bsz, T, A_KV_HEADS, HEAD_DIM), pos)
    o_a = windowed_gqa(q_a, k_a, v_a.reshape(Bsz, T, A_KV_HEADS, HEAD_DIM), a_sink)
    o_b = neighbourhood_attn(q_b.reshape(Bsz, T, B_HEADS, HEAD_DIM),
                             k_b.reshape(Bsz, T, B_HEADS, HEAD_DIM),
                             v_b.reshape(Bsz, T, B_HEADS, HEAD_DIM), na_rpb)
    o_m = memory_attn(q_m.reshape(Bsz, T, M_HEADS, HEAD_DIM), mem, w_mem_kv)
    gates = jax.nn.sigmoid(gate_logits.astype(jnp.float32) + b_gate.astype(jnp.float32))
    gates = gates.reshape(Bsz, T, N_BRANCH, D_MODEL)
    merged = (gates[:, :, 0] * (o_a @ w_proj_a)
              + gates[:, :, 1] * (o_b @ w_proj_b)
              + gates[:, :, 2] * (o_m @ w_proj_m))
    mixed = merged.astype(x.dtype) @ w_out
    x = layer_norm(DEEPNORM_ALPHA * x + mixed, ln1_g, ln1_b)
    ff = peer_ffn(x, w_peer_q, peer_keys1, peer_keys2, peer_u, peer_v)
    x = layer_norm(DEEPNORM_ALPHA * x + ff.astype(x.dtype), ln2_g, ln2_b)
    return x


def setup_inputs(seed: int = 0) -> dict:
    key = jax.random.key(seed)
    ks = jax.random.split(key, 24)

    def nrm(k, shape, scale):
        return jax.random.normal(k, shape, dtype=jnp.float32) * scale

    L = DEPTH
    col_scale = jnp.concatenate([
        jnp.ones((A_Q_W + A_KV_W,), jnp.float32),
        jnp.full((A_KV_W,), DEEPNORM_BETA, jnp.float32),
        jnp.ones((2 * B_W,), jnp.float32),
        jnp.full((B_W,), DEEPNORM_BETA, jnp.float32),
        jnp.ones((M_W + N_BRANCH * D_MODEL,), jnp.float32)])
    mem_scale = jnp.concatenate([jnp.ones((M_W,), jnp.float32),
                                 jnp.full((M_W,), DEEPNORM_BETA, jnp.float32)])
    return {
        "x_prompt": nrm(ks[0], (BATCH, SEQ, D_MODEL), 1.0),
        "x_sample": nrm(ks[1], (DEC_BATCH, DEC_SEQ, D_MODEL), 1.0),
        "mem_prompt": nrm(ks[2], (BATCH, N_MEM, D_MODEL), 1.0),
        "mem_sample": nrm(ks[3], (DEC_BATCH, N_MEM, D_MODEL), 1.0),
        "w_in": nrm(ks[4], (L, D_MODEL, IN_W), D_MODEL ** -0.5) * col_scale,
        "b_gate": nrm(ks[5], (L, N_BRANCH * D_MODEL), 0.1),
        "a_sink": nrm(ks[6], (L, A_HEADS), 1.0),
        "na_rpb": nrm(ks[7], (L, B_HEADS, 2 * NA_ROWS - 1, 2 * NA_COLS - 1), 0.5),
        "w_mem_kv": nrm(ks[8], (L, D_MODEL, 2 * M_W), D_MODEL ** -0.5) * mem_scale,
        "w_proj_a": nrm(ks[9], (L, A_Q_W, D_MODEL), DEEPNORM_BETA * A_Q_W ** -0.5),
        "w_proj_b": nrm(ks[10], (L, B_W, D_MODEL), DEEPNORM_BETA * B_W ** -0.5),
        "w_proj_m": nrm(ks[11], (L, M_W, D_MODEL), DEEPNORM_BETA * M_W ** -0.5),
        "w_out": nrm(ks[12], (L, D_MODEL, D_MODEL), DEEPNORM_BETA * D_MODEL ** -0.5),
        "ln1_g": 1.0 + nrm(ks[13], (L, D_MODEL), 0.02),
        "ln1_b": nrm(ks[14], (L, D_MODEL), 0.02),
        "w_peer_q": nrm(ks[15], (L, D_MODEL, PEER_HEADS * PEER_QDIM), D_MODEL ** -0.5),
        "peer_keys1": nrm(ks[16], (L, N_KEYS, PEER_HALF), PEER_HALF ** -0.5),
        "peer_keys2": nrm(ks[17], (L, N_KEYS, PEER_HALF), PEER_HALF ** -0.5),
        "peer_u": nrm(ks[18], (L, N_EXPERTS, D_MODEL), D_MODEL ** -0.5),
        "peer_v": nrm(ks[19], (L, N_EXPERTS, D_MODEL), DEEPNORM_BETA * PEER_HEADS ** -0.5),
        "ln2_g": 1.0 + nrm(ks[20], (L, D_MODEL), 0.02),
        "ln2_b": nrm(ks[21], (L, D_MODEL), 0.02),
    }


def reference(x_prompt, x_sample, mem_prompt, mem_sample, w_in, b_gate, a_sink, na_rpb, w_mem_kv,
              w_proj_a, w_proj_b, w_proj_m, w_out, ln1_g, ln1_b, w_peer_q, peer_keys1, peer_keys2,
              peer_u, peer_v, ln2_g, ln2_b):
    y_prompt = x_prompt
    y_sample = x_sample
    for l in range(DEPTH):
        layer_params = (w_in[l], b_gate[l], a_sink[l], na_rpb[l], w_mem_kv[l], w_proj_a[l],
                        w_proj_b[l], w_proj_m[l], w_out[l], ln1_g[l], ln1_b[l], w_peer_q[l],
                        peer_keys1[l], peer_keys2[l], peer_u[l], peer_v[l], ln2_g[l], ln2_b[l])
        y_prompt = encoder_layer(y_prompt, mem_prompt, *layer_params)
        y_sample = encoder_layer(y_sample, mem_sample, *layer_params)
    return (y_prompt, y_sample)
```

```python
import functools

import jax
import jax.numpy as jnp
from jax import lax
from jax.experimental import pallas as pl
from jax.experimental.pallas import tpu as pltpu

D_MODEL = 2048
HEAD_DIM = 128
A_HEADS = 8
A_KV_HEADS = 2
A_GROUP = A_HEADS // A_KV_HEADS
WINDOW = 128
ROPE_THETA = 500000.0
ROPE_DIM = HEAD_DIM // 4
B_HEADS = 4
GRID_W = 64
NA_ROWS = 8
NA_COLS = 16
M_HEADS = 4
N_BRANCH = 3
A_Q_W = A_HEADS * HEAD_DIM
A_KV_W = A_KV_HEADS * HEAD_DIM
B_W = B_HEADS * HEAD_DIM
M_W = M_HEADS * HEAD_DIM
QKV_W = A_Q_W + 2 * A_KV_W + 3 * B_W + M_W
PEER_HEADS = 8
PEER_QDIM = 256
PEER_HALF = PEER_QDIM // 2
N_KEYS = 128
PEER_TOPK = 16
LN_EPS = 1e-5
NEG_INF = -1e30
ATTN_SCALE = HEAD_DIM ** -0.5

LANES = 128
SUBLANES = 8
VMEM_LIMIT = 56 * 1024 * 1024

BF16 = jnp.bfloat16
F32 = jnp.float32

_PROJ_PIECES = (("q_a", A_Q_W, True), ("k_a", A_KV_W, True), ("v_a", A_KV_W, False),
                ("q_b", B_W, False), ("k_b", B_W, False), ("v_b", B_W, False), ("q_m", M_W, False))
_COL_CHUNK = 512


def _dot(a, b):
    return jnp.dot(a, b, preferred_element_type=F32)


def _dot_nt(a, b):
    return lax.dot_general(a, b, (((1,), (1,)), ((), ())), preferred_element_type=F32)


def _params(*semantics):
    return pltpu.CompilerParams(dimension_semantics=semantics, vmem_limit_bytes=VMEM_LIMIT)


def _resident(shape):
    return pl.BlockSpec(shape, lambda *_: (0,) * len(shape), pipeline_mode=pl.Buffered(1))


def _layer_norm(h, g, b):
    mu = jnp.mean(h, axis=-1, keepdims=True)
    hc = h - mu
    var = jnp.mean(hc * hc, axis=-1, keepdims=True)
    return hc * lax.rsqrt(var + LN_EPS) * g + b


def _proj_kernel(x_ref, w_ref, cos_ref, sin_up_ref, sin_dn_ref, *out_refs):
    xb = x_ref[...].astype(BF16)
    off = 0
    for (_, width, rotary), o_ref in zip(_PROJ_PIECES, out_refs):
        for c0 in range(0, width, _COL_CHUNK):
            cw = min(_COL_CHUNK, width - c0)
            acc = _dot(xb, w_ref[:, off + c0:off + c0 + cw])
            if rotary:
                reps = cw // HEAD_DIM
                cos = jnp.concatenate([cos_ref[...]] * reps, axis=1)
                sin_up = jnp.concatenate([sin_up_ref[...]] * reps, axis=1)
                sin_dn = jnp.concatenate([sin_dn_ref[...]] * reps, axis=1)
                half = ROPE_DIM // 2
                acc = (acc * cos + pltpu.roll(acc, half, 1) * sin_up
                       + pltpu.roll(acc, cw - half, 1) * sin_dn)
            o_ref[:, c0:c0 + cw] = acc.astype(o_ref.dtype)
        off += width


def _rope_tables(seq_len):
    half = ROPE_DIM // 2
    inv_freq = ROPE_THETA ** (-jnp.arange(half, dtype=F32) * 2.0 / ROPE_DIM)
    ang = jnp.arange(seq_len, dtype=F32)[:, None] * inv_freq[None, :]
    cos, sin = jnp.cos(ang), jnp.sin(ang)
    zeros = jnp.zeros((seq_len, HEAD_DIM - ROPE_DIM), F32)
    zh = jnp.zeros((seq_len, half), F32)
    cos_t = jnp.concatenate([cos, cos, zeros + 1.0], axis=1)
    sin_up = jnp.concatenate([zh, sin, zeros], axis=1)
    sin_dn = jnp.concatenate([-sin, zh, zeros], axis=1)
    return cos_t, sin_up, sin_dn


def _project(x2, w_qkv, rope, seq_len, tm=512):
    n = x2.shape[0]
    blocks_per_seq = seq_len // tm
    tab_spec = pl.BlockSpec((tm, HEAD_DIM), lambda i: (i % blocks_per_seq, 0))
    return pl.pallas_call(
        _proj_kernel,
        grid=(n // tm,),
        in_specs=[pl.BlockSpec((tm, D_MODEL), lambda i: (i, 0)),
                  _resident((D_MODEL, QKV_W)), tab_spec, tab_spec, tab_spec],
        out_specs=[pl.BlockSpec((tm, w), lambda i: (i, 0)) for _, w, _ in _PROJ_PIECES],
        out_shape=[jax.ShapeDtypeStruct((n, w), BF16) for _, w, _ in _PROJ_PIECES],
        compiler_params=_params("parallel"),
        name="qkv_proj",
    )(x2, w_qkv, *rope)


_WIN_CHUNK = 512
_WIN_SUB = _WIN_CHUNK // WINDOW


def _win_attn_kernel(sink_ref, q_ref, kp_ref, kc_ref, kn_ref, vp_ref, vc_ref, vn_ref, o_ref):
    c = pl.program_id(1)
    rows = A_GROUP * WINDOW
    qi = lax.broadcasted_iota(jnp.int32, (rows, WINDOW), 0) % WINDOW
    kj = lax.broadcasted_iota(jnp.int32, (rows, WINDOW), 1)
    below = kj >= qi
    above = kj <= qi
    prev_pad = jnp.where(c > 0, 0.0, NEG_INF)
    next_pad = jnp.where(c < pl.num_programs(1) - 1, 0.0, NEG_INF)
    for g in range(A_KV_HEADS):
        hs = slice(g * HEAD_DIM, (g + 1) * HEAD_DIM)
        sink = jnp.concatenate(
            [jnp.full((WINDOW, 1), sink_ref[A_GROUP * g + u], F32) for u in range(A_GROUP)], axis=0)
        for sb in range(_WIN_SUB):
            rs = slice(sb * WINDOW, (sb + 1) * WINDOW)
            q = jnp.concatenate(
                [q_ref[rs, (A_GROUP * g + u) * HEAD_DIM:(A_GROUP * g + u + 1) * HEAD_DIM]
                 for u in range(A_GROUP)], axis=0)
            if sb == 0:
                k_prev, v_prev, pad_prev = kp_ref[:, hs], vp_ref[:, hs], prev_pad
            else:
                ps = slice((sb - 1) * WINDOW, sb * WINDOW)
                k_prev, v_prev, pad_prev = kc_ref[ps, hs], vc_ref[ps, hs], 0.0
            if sb == _WIN_SUB - 1:
                k_next, v_next, pad_next = kn_ref[:, hs], vn_ref[:, hs], next_pad
            else:
                ns = slice((sb + 1) * WINDOW, (sb + 2) * WINDOW)
                k_next, v_next, pad_next = kc_ref[ns, hs], vc_ref[ns, hs], 0.0
            s_prev = jnp.where(below, _dot_nt(q, k_prev) * ATTN_SCALE, NEG_INF) + pad_prev
            s_cur = _dot_nt(q, kc_ref[rs, hs]) * ATTN_SCALE
            s_next = jnp.where(above, _dot_nt(q, k_next) * ATTN_SCALE, NEG_INF) + pad_next
            m = jnp.max(jnp.maximum(jnp.maximum(s_prev, s_cur), s_next), axis=-1, keepdims=True)
            m = jnp.maximum(m, sink)
            p_prev, p_cur, p_next = jnp.exp(s_prev - m), jnp.exp(s_cur - m), jnp.exp(s_next - m)
            denom = jnp.sum(p_prev + p_cur + p_next, axis=-1, keepdims=True) + jnp.exp(sink - m)
            o = (_dot(p_prev.astype(BF16), v_prev) + _dot(p_cur.astype(BF16), vc_ref[rs, hs])
                 + _dot(p_next.astype(BF16), v_next)) / denom
            for u in range(A_GROUP):
                h = A_GROUP * g + u
                o_ref[rs, h * HEAD_DIM:(h + 1) * HEAD_DIM] = (
                    o[u * WINDOW:(u + 1) * WINDOW].astype(o_ref.dtype))


def _window_attention(q, k, v, sink, batch, seq_len):
    n = q.shape[0]
    chunks = seq_len // _WIN_CHUNK
    blocks = seq_len // WINDOW

    def cur(b, c):
        return (b * chunks + c, 0)

    def prev(b, c):
        return (b * blocks + jnp.maximum(c * _WIN_SUB - 1, 0), 0)

    def nxt(b, c):
        return (b * blocks + jnp.minimum((c + 1) * _WIN_SUB, blocks - 1), 0)

    def kv_specs():
        return [pl.BlockSpec((WINDOW, A_KV_W), prev), pl.BlockSpec((_WIN_CHUNK, A_KV_W), cur),
                pl.BlockSpec((WINDOW, A_KV_W), nxt)]

    return pl.pallas_call(
        _win_attn_kernel,
        grid=(batch, chunks),
        in_specs=[pl.BlockSpec(memory_space=pltpu.SMEM), pl.BlockSpec((_WIN_CHUNK, A_Q_W), cur),
                  *kv_specs(), *kv_specs()],
        out_specs=pl.BlockSpec((_WIN_CHUNK, A_Q_W), cur),
        out_shape=jax.ShapeDtypeStruct((n, A_Q_W), BF16),
        compiler_params=_params("parallel", "parallel"),
        name="window_attn",
    )(sink, q, k, k, k, v, v, v)


_NA_KEYS = NA_ROWS * GRID_W
_NA_CHUNK = NA_ROWS * GRID_W
_NA_HALO = _NA_CHUNK // 2


def _na_bias_table(rpb):
    pat = jnp.arange(NA_ROWS)[:, None]
    dr = jnp.arange(NA_ROWS)[None, :] - pat
    c = jnp.arange(GRID_W)
    cs = jnp.clip(c - NA_COLS // 2, 0, GRID_W - NA_COLS)
    col_ok = (c[None, :] >= cs[:, None]) & (c[None, :] < cs[:, None] + NA_COLS)
    dc = jnp.clip(c[None, :] - c[:, None], -(NA_COLS - 1), NA_COLS - 1)
    bias = rpb[:, (dr + NA_ROWS - 1)[:, None, :, None], (dc + NA_COLS - 1)[None, :, None, :]]
    bias = jnp.where(col_ok[None, None, :, None, :], bias.astype(F32), NEG_INF)
    return bias.reshape(B_HEADS, NA_ROWS, GRID_W, _NA_KEYS).transpose(1, 0, 2, 3)


def _na_kernel(bias_ref, q_ref, kp_ref, kc_ref, kn_ref, vp_ref, vc_ref, vn_ref, o_ref,
               kbuf, vbuf, *, grid_rows):
    c = pl.program_id(1)
    kbuf[0:_NA_HALO] = kp_ref[...]
    kbuf[_NA_HALO:_NA_HALO + _NA_CHUNK] = kc_ref[...]
    kbuf[_NA_HALO + _NA_CHUNK:] = kn_ref[...]
    vbuf[0:_NA_HALO] = vp_ref[...]
    vbuf[_NA_HALO:_NA_HALO + _NA_CHUNK] = vc_ref[...]
    vbuf[_NA_HALO + _NA_CHUNK:] = vn_ref[...]
    halo_rows = _NA_HALO // GRID_W
    for i in range(NA_ROWS):
        r = c * NA_ROWS + i
        rs = jnp.clip(r - NA_ROWS // 2, 0, grid_rows - NA_ROWS)
        pat = r - rs
        start = pl.multiple_of((rs - (c * NA_ROWS - halo_rows)) * GRID_W, GRID_W)
        qs = slice(i * GRID_W, (i + 1) * GRID_W)
        for h in range(B_HEADS):
            hs = slice(h * HEAD_DIM, (h + 1) * HEAD_DIM)
            kw = kbuf[pl.ds(start, _NA_KEYS), hs]
            vw = vbuf[pl.ds(start, _NA_KEYS), hs]
            s = _dot_nt(q_ref[qs, hs], kw) * ATTN_SCALE + bias_ref[pat, h]
            m = jnp.max(s, axis=-1, keepdims=True)
            p = jnp.exp(s - m)
            denom = jnp.sum(p, axis=-1, keepdims=True)
            o_ref[qs, hs] = (_dot(p.astype(BF16), vw) / denom).astype(o_ref.dtype)


def _neighbourhood_attention(q, k, v, bias, batch, seq_len):
    n = q.shape[0]
    grid_rows = seq_len // GRID_W
    assert grid_rows >= 2 * NA_ROWS and seq_len % _NA_CHUNK == 0
    chunks = seq_len // _NA_CHUNK
    halos = seq_len // _NA_HALO

    def cur(b, c):
        return (b * chunks + c, 0)

    def prev(b, c):
        return (b * halos + jnp.maximum(2 * c - 1, 0), 0)

    def nxt(b, c):
        return (b * halos + jnp.minimum(2 * c + 2, halos - 1), 0)

    def kv_specs():
        return [pl.BlockSpec((_NA_HALO, B_W), prev), pl.BlockSpec((_NA_CHUNK, B_W), cur),
                pl.BlockSpec((_NA_HALO, B_W), nxt)]

    buf = pltpu.VMEM((_NA_CHUNK + 2 * _NA_HALO, B_W), BF16)
    return pl.pallas_call(
        functools.partial(_na_kernel, grid_rows=grid_rows),
        grid=(batch, chunks),
        in_specs=[_resident((NA_ROWS, B_HEADS, GRID_W, _NA_KEYS)),
                  pl.BlockSpec((_NA_CHUNK, B_W), cur), *kv_specs(), *kv_specs()],
        out_specs=pl.BlockSpec((_NA_CHUNK, B_W), cur),
        out_shape=jax.ShapeDtypeStruct((n, B_W), BF16),
        scratch_shapes=[buf, buf],
        compiler_params=_params("parallel", "parallel"),
        name="neighbourhood_attn",
    )(bias, q, k, k, k, v, v, v)


def _mem_kv_kernel(mem_ref, w_ref, k_ref, v_ref):
    kv = _dot(mem_ref[...].astype(BF16), w_ref[...])
    k_ref[...] = kv[:, :M_W].astype(k_ref.dtype)
    v_ref[...] = kv[:, M_W:].astype(v_ref.dtype)


def _memory_kv(mem2, w_mem_kv, tm=256):
    n = mem2.shape[0]
    out = jax.ShapeDtypeStruct((n, M_W), BF16)
    return pl.pallas_call(
        _mem_kv_kernel,
        grid=(n // tm,),
        in_specs=[pl.BlockSpec((tm, D_MODEL), lambda i: (i, 0)), _resident((D_MODEL, 2 * M_W))],
        out_specs=[pl.BlockSpec((tm, M_W), lambda i: (i, 0))] * 2,
        out_shape=[out, out],
        compiler_params=_params("parallel"),
        name="memory_kv",
    )(mem2, w_mem_kv)


def _mem_attn_kernel(q_ref, k_ref, v_ref, o_ref):
    for h in range(M_HEADS):
        hs = slice(h * HEAD_DIM, (h + 1) * HEAD_DIM)
        s = _dot_nt(q_ref[:, hs], k_ref[:, hs]) * ATTN_SCALE
        m = jnp.max(s, axis=-1, keepdims=True)
        p = jnp.exp(s - m)
        denom = jnp.sum(p, axis=-1, keepdims=True)
        o_ref[:, hs] = (_dot(p.astype(BF16), v_ref[:, hs]) / denom).astype(o_ref.dtype)


def _memory_attention(q, km, vm, batch, seq_len, tm=512):
    n = q.shape[0]
    n_mem = km.shape[0] // batch
    chunks = seq_len // tm
    return pl.pallas_call(
        _mem_attn_kernel,
        grid=(batch, chunks),
        in_specs=[pl.BlockSpec((tm, M_W), lambda b, c: (b * chunks + c, 0)),
                  pl.BlockSpec((n_mem, M_W), lambda b, c: (b, 0)),
                  pl.BlockSpec((n_mem, M_W), lambda b, c: (b, 0))],
        out_specs=pl.BlockSpec((tm, M_W), lambda b, c: (b * chunks + c, 0)),
        out_shape=jax.ShapeDtypeStruct((n, M_W), BF16),
        compiler_params=_params("parallel", "parallel"),
        name="memory_attn",
    )(q, km, vm)


_MERGE_COLS = 512


def _merge_kernel(x_ref, oa_ref, ob_ref, om_ref, wg0_ref, wg1_ref, wg2_ref, bg0_ref, bg1_ref,
                  bg2_ref, wa_ref, wb_ref, wm_ref, wo_ref, g_ref, b_ref, y_ref, yb_ref,
                  xb_ref, acc_ref, *, alpha):
    j = pl.program_id(1)

    @pl.when(j == 0)
    def _():
        xb_ref[...] = x_ref[...].astype(BF16)
        acc_ref[...] = jnp.zeros_like(acc_ref)

    xb = xb_ref[...]
    merged = (jax.nn.sigmoid(_dot(xb, wg0_ref[...]) + bg0_ref[...]) * _dot(oa_ref[...], wa_ref[...])
              + jax.nn.sigmoid(_dot(xb, wg1_ref[...]) + bg1_ref[...]) * _dot(ob_ref[...], wb_ref[...])
              + jax.nn.sigmoid(_dot(xb, wg2_ref[...]) + bg2_ref[...]) * _dot(om_ref[...], wm_ref[...]))
    acc_ref[...] += _dot(merged.astype(BF16), wo_ref[...])

    @pl.when(j == pl.num_programs(1) - 1)
    def _():
        y = _layer_norm(alpha * x_ref[...] + acc_ref[...], g_ref[...], b_ref[...])
        y_ref[...] = y
        yb_ref[...] = y.astype(BF16)


def _merge(x2, o_a, o_b, o_m, w_gate, b_gate, w_pa, w_pb, w_pm, w_out, ln_g, ln_b, alpha, tm=512):
    n = x2.shape[0]
    steps = D_MODEL // _MERGE_COLS
    row = lambda i, j: (i, 0)

    def gate_w(branch):
        return pl.BlockSpec((D_MODEL, _MERGE_COLS), lambda i, j: (0, branch * steps + j))

    def gate_b(branch):
        return pl.BlockSpec((1, _MERGE_COLS), lambda i, j: (0, branch * steps + j))

    col = lambda i, j: (0, j)
    vec = pl.BlockSpec((1, D_MODEL), lambda i, j: (0, 0))

    def once(shape):
        return pl.BlockSpec(shape, row, pipeline_mode=pl.Buffered(1))

    return pl.pallas_call(
        functools.partial(_merge_kernel, alpha=alpha),
        grid=(n // tm, steps),
        in_specs=[once((tm, D_MODEL)), once((tm, A_Q_W)), once((tm, B_W)), once((tm, M_W)),
                  gate_w(0), gate_w(1), gate_w(2), gate_b(0), gate_b(1), gate_b(2),
                  pl.BlockSpec((A_Q_W, _MERGE_COLS), col), pl.BlockSpec((B_W, _MERGE_COLS), col),
                  pl.BlockSpec((M_W, _MERGE_COLS), col),
                  pl.BlockSpec((_MERGE_COLS, D_MODEL), lambda i, j: (j, 0)), vec, vec],
        out_specs=[pl.BlockSpec((tm, D_MODEL), row), pl.BlockSpec((tm, D_MODEL), row)],
        out_shape=[jax.ShapeDtypeStruct((n, D_MODEL), F32), jax.ShapeDtypeStruct((n, D_MODEL), BF16)],
        scratch_shapes=[pltpu.VMEM((tm, D_MODEL), BF16), pltpu.VMEM((tm, D_MODEL), F32)],
        compiler_params=_params("parallel", "arbitrary"),
        name="merge_out_ln",
    )(x2, o_a, o_b, o_m, w_gate, w_gate, w_gate, b_gate, b_gate, b_gate, w_pa, w_pb, w_pm, w_out,
      ln_g, ln_b)


def _take_top(s, count):
    rows = s.shape[0]
    row_id = lax.broadcasted_iota(jnp.int32, s.shape, 0).astype(F32)
    order = jnp.full(s.shape, float(count), F32)
    vals, poss = [], []
    for a in range(count):
        m = jnp.max(s, axis=0, keepdims=True)
        pos = jnp.min(jnp.where(s == m, row_id, float(rows)), axis=0, keepdims=True)
        hit = row_id == pos
        order = jnp.where(hit, float(a), order)
        s = jnp.where(hit, -jnp.inf, s)
        vals.append(m)
        poss.append(pos)
    return vals, poss, order


def _peer_prep_kernel(xb_ref, wq_ref, k1_ref, k2_ref, n1_ref, e1_ref, r2_ref, e2_ref, st_ref):
    tm = xb_ref.shape[0]
    xb = xb_ref[...]
    for h in range(PEER_HEADS):
        q = _dot(xb, wq_ref[:, h * PEER_QDIM:(h + 1) * PEER_QDIM]).astype(BF16)
        st_ref[2 * h] = _dot_nt(k1_ref[...], q[:, :PEER_HALF])
        st_ref[2 * h + 1] = _dot_nt(k2_ref[...], q[:, PEER_HALF:])
    lane_chunks = tm // LANES

    def body(it, carry):
        h = it // lane_chunks
        lanes = pl.ds(pl.multiple_of((it % lane_chunks) * LANES, LANES), LANES)
        s1 = st_ref[2 * h, :, lanes]
        s2 = st_ref[2 * h + 1, :, lanes]
        t1, _, order1 = _take_top(s1, PEER_TOPK)
        t2, _, order2 = _take_top(s2, PEER_TOPK)
        t2_all = jnp.concatenate(t2, axis=0)
        cand = jnp.concatenate([t1[a] + t2_all for a in range(PEER_TOPK)], axis=0)
        top, pos, _ = _take_top(cand, PEER_TOPK)
        a_id = lax.broadcasted_iota(jnp.int32, (PEER_TOPK, LANES), 0).astype(F32)
        taken = jnp.zeros((PEER_TOPK, LANES), F32)
        z = jnp.zeros((1, LANES), F32)
        for k in range(PEER_TOPK):
            taken = taken + jnp.where(a_id == jnp.floor(pos[k] * (1.0 / PEER_TOPK)), 1.0, 0.0)
            z = z + jnp.exp(top[k] - top[0])
        n1 = jnp.zeros((N_KEYS, LANES), F32)
        for a in range(PEER_TOPK):
            n1 = jnp.where(order1 == float(a), taken[a:a + 1, :], n1)
        n1_ref[h, :, lanes] = n1
        e1_ref[h, :, lanes] = jnp.exp(s1 - t1[0])
        r2_ref[h, :, lanes] = order2
        e2_ref[h, :, lanes] = jnp.exp(s2 - t2[0]) / z
        return carry

    lax.fori_loop(0, PEER_HEADS * lane_chunks, body, 0)


def _peer_prep(x1b, w_q, keys1, keys2, tm=512):
    n = x1b.shape[0]
    out = jax.ShapeDtypeStruct((PEER_HEADS, N_KEYS, n), F32)
    spec = pl.BlockSpec((PEER_HEADS, N_KEYS, tm), lambda i: (0, 0, i))
    return pl.pallas_call(
        _peer_prep_kernel,
        grid=(n // tm,),
        in_specs=[pl.BlockSpec((tm, D_MODEL), lambda i: (i, 0)),
                  _resident((D_MODEL, PEER_HEADS * PEER_QDIM)),
                  _resident((N_KEYS, PEER_HALF)), _resident((N_KEYS, PEER_HALF))],
        out_specs=[spec] * 4,
        out_shape=[out] * 4,
        scratch_shapes=[pltpu.VMEM((2 * PEER_HEADS, N_KEYS, tm), F32)],
        compiler_params=_params("parallel"),
        name="peer_retrieve",
    )(x1b, w_q, keys1, keys2)


def _peer_kernel(xb_ref, u_ref, vt_ref, n1_ref, e1_ref, r2_ref, e2_ref, x_ref, g_ref, b_ref,
                 y_ref, acc_ref, a_ref, w_ref, *, alpha):
    j = pl.program_id(1)
    te, tm = a_ref.shape

    @pl.when(j == 0)
    def _():
        acc_ref[...] = jnp.zeros_like(acc_ref)

    a_ref[...] = _dot_nt(u_ref[...], xb_ref[...])
    keys_per_step = te // N_KEYS
    group = pl.ds(pl.multiple_of((j * keys_per_step) // SUBLANES * SUBLANES, SUBLANES), SUBLANES)
    upper = (j * keys_per_step) % SUBLANES >= keys_per_step

    def first_key_row(ref, h, sub, ls):
        rows = ref[h, group, ls]
        if keys_per_step == SUBLANES:
            return rows[sub:sub + 1]
        return jnp.where(upper, rows[keys_per_step + sub:keys_per_step + sub + 1], rows[sub:sub + 1])

    for sub in range(keys_per_step):
        es = slice(sub * N_KEYS, (sub + 1) * N_KEYS)
        for tc in range(tm // LANES):
            ls = slice(tc * LANES, (tc + 1) * LANES)
            gate = jnp.zeros((N_KEYS, LANES), F32)
            for h in range(PEER_HEADS):
                n1 = first_key_row(n1_ref, h, sub, ls)
                e1 = first_key_row(e1_ref, h, sub, ls)
                gate = gate + jnp.where(r2_ref[h, :, ls] < n1, e2_ref[h, :, ls] * e1, 0.0)
            a = a_ref[es, ls]
            gelu = 0.5 * a * (1.0 + lax.erf(a * (2.0 ** -0.5)))
            w_ref[es, ls] = (gelu * gate).astype(BF16)
    acc_ref[...] += _dot(vt_ref[...], w_ref[...])

    @pl.when(j == pl.num_programs(1) - 1)
    def _():
        ff = acc_ref[...].T
        y_ref[...] = _layer_norm(alpha * x_ref[...] + ff, g_ref[...], b_ref[...])


def _peer(x1, x1b, u, vt, n1, e1, r2, e2, ln_g, ln_b, alpha, tm=512, te=512):
    n = x1.shape[0]
    n_experts = u.shape[0]
    assert te // N_KEYS in (SUBLANES // 2, SUBLANES)
    row = lambda i, j: (i, 0)
    sel = pl.BlockSpec((PEER_HEADS, N_KEYS, tm), lambda i, j: (0, 0, i),
                       pipeline_mode=pl.Buffered(1))
    vec = pl.BlockSpec((1, D_MODEL), lambda i, j: (0, 0))
    return pl.pallas_call(
        functools.partial(_peer_kernel, alpha=alpha),
        grid=(n // tm, n_experts // te),
        in_specs=[pl.BlockSpec((tm, D_MODEL), row),
                  pl.BlockSpec((te, D_MODEL), lambda i, j: (j, 0)),
                  pl.BlockSpec((D_MODEL, te), lambda i, j: (0, j)),
                  sel, sel, sel, sel,
                  pl.BlockSpec((tm, D_MODEL), row, pipeline_mode=pl.Buffered(1)), vec, vec],
        out_specs=pl.BlockSpec((tm, D_MODEL), row),
        out_shape=jax.ShapeDtypeStruct((n, D_MODEL), F32),
        scratch_shapes=[pltpu.VMEM((D_MODEL, tm), F32), pltpu.VMEM((te, tm), F32),
                        pltpu.VMEM((te, tm), BF16)],
        compiler_params=_params("parallel", "arbitrary"),
        name="peer_experts_ln",
    )(x1b, u, vt, n1, e1, r2, e2, x1, ln_g, ln_b)


def _encoder_layer(x, mem, p, alpha):
    batch, seq_len, _ = x.shape
    x2 = x.reshape(batch * seq_len, D_MODEL)
    q_a, k_a, v_a, q_b, k_b, v_b, q_m = _project(x2, p["w_qkv"], _rope_tables(seq_len), seq_len)
    o_a = _window_attention(q_a, k_a, v_a, p["a_sink"], batch, seq_len)
    o_b = _neighbourhood_attention(q_b, k_b, v_b, p["na_bias"], batch, seq_len)
    km, vm = _memory_kv(mem.reshape(-1, D_MODEL), p["w_mem_kv"])
    o_m = _memory_attention(q_m, km, vm, batch, seq_len)
    x1, x1b = _merge(x2, o_a, o_b, o_m, p["w_gate"], p["b_gate"], p["w_proj_a"], p["w_proj_b"],
                     p["w_proj_m"], p["w_out"], p["ln1_g"], p["ln1_b"], alpha)
    n1, e1, r2, e2 = _peer_prep(x1b, p["w_peer_q"], p["peer_keys1"], p["peer_keys2"])
    y = _peer(x1, x1b, p["peer_u"], p["peer_vt"], n1, e1, r2, e2, p["ln2_g"], p["ln2_b"], alpha)
    return y.reshape(batch, seq_len, D_MODEL)


def _layer_params(w_in, b_gate, a_sink, na_rpb, w_mem_kv, w_proj_a, w_proj_b, w_proj_m, w_out,
                  ln1_g, ln1_b, w_peer_q, peer_keys1, peer_keys2, peer_u, peer_v, ln2_g, ln2_b):
    row = lambda v: v.reshape(1, -1).astype(F32)
    return {
        "w_qkv": w_in[:, :QKV_W].astype(BF16),
        "w_gate": w_in[:, QKV_W:].astype(BF16),
        "b_gate": row(b_gate),
        "a_sink": a_sink.astype(F32),
        "na_bias": _na_bias_table(na_rpb),
        "w_mem_kv": w_mem_kv.astype(BF16),
        "w_proj_a": w_proj_a.astype(BF16),
        "w_proj_b": w_proj_b.astype(BF16),
        "w_proj_m": w_proj_m.astype(BF16),
        "w_out": w_out.astype(BF16),
        "ln1_g": row(ln1_g),
        "ln1_b": row(ln1_b),
        "w_peer_q": w_peer_q.astype(BF16),
        "peer_keys1": peer_keys1.astype(BF16),
        "peer_keys2": peer_keys2.astype(BF16),
        "peer_u": peer_u.astype(BF16),
        "peer_vt": peer_v.T.astype(BF16),
        "ln2_g": row(ln2_g),
        "ln2_b": row(ln2_b),
    }


def kernel(x_prompt, x_sample, mem_prompt, mem_sample, w_in, b_gate, a_sink, na_rpb, w_mem_kv,
           w_proj_a, w_proj_b, w_proj_m, w_out, ln1_g, ln1_b, w_peer_q, peer_keys1, peer_keys2,
           peer_u, peer_v, ln2_g, ln2_b):
    depth = w_in.shape[0]
    alpha = (2.0 * depth) ** 0.25
    y_prompt, y_sample = x_prompt, x_sample
    for l in range(depth):
        p = _layer_params(w_in[l], b_gate[l], a_sink[l], na_rpb[l], w_mem_kv[l], w_proj_a[l],
                          w_proj_b[l], w_proj_m[l], w_out[l], ln1_g[l], ln1_b[l], w_peer_q[l],
                          peer_keys1[l], peer_keys2[l], peer_u[l], peer_v[l], ln2_g[l], ln2_b[l])
        y_prompt = _encoder_layer(y_prompt, mem_prompt, p, alpha)
        y_sample = _encoder_layer(y_sample, mem_sample, p, alpha)
    return (y_prompt, y_sample)
```

```python
import functools

import jax
import jax.numpy as jnp
from jax import lax
from jax.experimental import pallas as pl
from jax.experimental.pallas import tpu as pltpu

D_MODEL = 2048
HEAD_DIM = 128
A_HEADS = 8
A_KV_HEADS = 2
A_GROUP = A_HEADS // A_KV_HEADS
WINDOW = 128
ROPE_THETA = 500000.0
ROPE_DIM = HEAD_DIM // 4
B_HEADS = 4
GRID_W = 64
NA_ROWS = 8
NA_COLS = 16
M_HEADS = 4
N_BRANCH = 3
A_Q_W = A_HEADS * HEAD_DIM
A_KV_W = A_KV_HEADS * HEAD_DIM
B_W = B_HEADS * HEAD_DIM
M_W = M_HEADS * HEAD_DIM
QKV_W = A_Q_W + 2 * A_KV_W + 3 * B_W + M_W
PEER_HEADS = 8
PEER_QDIM = 256
PEER_HALF = PEER_QDIM // 2
N_KEYS = 128
PEER_TOPK = 16
LN_EPS = 1e-5
NEG_INF = -1e30
ATTN_SCALE = HEAD_DIM ** -0.5

LANES = 128
SUBLANES = 8
VMEM_LIMIT = 56 * 1024 * 1024

BF16 = jnp.bfloat16
F32 = jnp.float32

_PROJ_PIECES = (("q_a", A_Q_W, True), ("k_a", A_KV_W, True), ("v_a", A_KV_W, False),
                ("q_b", B_W, False), ("k_b", B_W, False), ("v_b", B_W, False), ("q_m", M_W, False))
_COL_CHUNK = 512


def _dot(a, b):
    return jnp.dot(a, b, preferred_element_type=F32)


def _dot_nt(a, b):
    return lax.dot_general(a, b, (((1,), (1,)), ((), ())), preferred_element_type=F32)


def _params(*semantics):
    return pltpu.CompilerParams(dimension_semantics=semantics, vmem_limit_bytes=VMEM_LIMIT)


def _resident(shape):
    return pl.BlockSpec(shape, lambda *_: (0,) * len(shape), pipeline_mode=pl.Buffered(1))


def _layer_norm(h, g, b):
    mu = jnp.mean(h, axis=-1, keepdims=True)
    hc = h - mu
    var = jnp.mean(hc * hc, axis=-1, keepdims=True)
    return hc * lax.rsqrt(var + LN_EPS) * g + b


def _proj_kernel(x_ref, w_ref, cos_ref, sin_up_ref, sin_dn_ref, *out_refs):
    xb = x_ref[...].astype(BF16)
    off = 0
    for (_, width, rotary), o_ref in zip(_PROJ_PIECES, out_refs):
        for c0 in range(0, width, _COL_CHUNK):
            cw = min(_COL_CHUNK, width - c0)
            acc = _dot(xb, w_ref[:, off + c0:off + c0 + cw])
            if rotary:
                reps = cw // HEAD_DIM
                cos = jnp.concatenate([cos_ref[...]] * reps, axis=1)
                sin_up = jnp.concatenate([sin_up_ref[...]] * reps, axis=1)
                sin_dn = jnp.concatenate([sin_dn_ref[...]] * reps, axis=1)
                half = ROPE_DIM // 2
                acc = (acc * cos + pltpu.roll(acc, half, 1) * sin_up
                       + pltpu.roll(acc, cw - half, 1) * sin_dn)
            o_ref[:, c0:c0 + cw] = acc.astype(o_ref.dtype)
        off += width


def _rope_tables(seq_len):
    half = ROPE_DIM // 2
    inv_freq = ROPE_THETA ** (-jnp.arange(half, dtype=F32) * 2.0 / ROPE_DIM)
    ang = jnp.arange(seq_len, dtype=F32)[:, None] * inv_freq[None, :]
    cos, sin = jnp.cos(ang), jnp.sin(ang)
    zeros = jnp.zeros((seq_len, HEAD_DIM - ROPE_DIM), F32)
    zh = jnp.zeros((seq_len, half), F32)
    cos_t = jnp.concatenate([cos, cos, zeros + 1.0], axis=1)
    sin_up = jnp.concatenate([zh, sin, zeros], axis=1)
    sin_dn = jnp.concatenate([-sin, zh, zeros], axis=1)
    return cos_t, sin_up, sin_dn


def _project(x2, w_qkv, rope, seq_len, tm=512):
    n = x2.shape[0]
    blocks_per_seq = seq_len // tm
    tab_spec = pl.BlockSpec((tm, HEAD_DIM), lambda i: (i % blocks_per_seq, 0))
    return pl.pallas_call(
        _proj_kernel,
        grid=(n // tm,),
        in_specs=[pl.BlockSpec((tm, D_MODEL), lambda i: (i, 0)),
                  _resident((D_MODEL, QKV_W)), tab_spec, tab_spec, tab_spec],
        out_specs=[pl.BlockSpec((tm, w), lambda i: (i, 0)) for _, w, _ in _PROJ_PIECES],
        out_shape=[jax.ShapeDtypeStruct((n, w), BF16) for _, w, _ in _PROJ_PIECES],
        compiler_params=_params("parallel"),
        name="qkv_proj",
    )(x2, w_qkv, *rope)


_WIN_CHUNK = 512
_WIN_SUB = _WIN_CHUNK // WINDOW


def _win_attn_kernel(sink_ref, q_ref, kp_ref, kc_ref, kn_ref, vp_ref, vc_ref, vn_ref, o_ref):
    c = pl.program_id(1)
    rows = A_GROUP * WINDOW
    qi = lax.broadcasted_iota(jnp.int32, (rows, WINDOW), 0) % WINDOW
    kj = lax.broadcasted_iota(jnp.int32, (rows, WINDOW), 1)
    below = kj >= qi
    above = kj <= qi
    prev_pad = jnp.where(c > 0, 0.0, NEG_INF)
    next_pad = jnp.where(c < pl.num_programs(1) - 1, 0.0, NEG_INF)
    for g in range(A_KV_HEADS):
        hs = slice(g * HEAD_DIM, (g + 1) * HEAD_DIM)
        sink = jnp.concatenate(
            [jnp.full((WINDOW, 1), sink_ref[A_GROUP * g + u], F32) for u in range(A_GROUP)], axis=0)
        for sb in range(_WIN_SUB):
            rs = slice(sb * WINDOW, (sb + 1) * WINDOW)
            q = jnp.concatenate(
                [q_ref[rs, (A_GROUP * g + u) * HEAD_DIM:(A_GROUP * g + u + 1) * HEAD_DIM]
                 for u in range(A_GROUP)], axis=0)
            if sb == 0:
                k_prev, v_prev, pad_prev = kp_ref[:, hs], vp_ref[:, hs], prev_pad
            else:
                ps = slice((sb - 1) * WINDOW, sb * WINDOW)
                k_prev, v_prev, pad_prev = kc_ref[ps, hs], vc_ref[ps, hs], 0.0
            if sb == _WIN_SUB - 1:
                k_next, v_next, pad_next = kn_ref[:, hs], vn_ref[:, hs], next_pad
            else:
                ns = slice((sb + 1) * WINDOW, (sb + 2) * WINDOW)
                k_next, v_next, pad_next = kc_ref[ns, hs], vc_ref[ns, hs], 0.0
            s_prev = jnp.where(below, _dot_nt(q, k_prev) * ATTN_SCALE, NEG_INF) + pad_prev
            s_cur = _dot_nt(q, kc_ref[rs, hs]) * ATTN_SCALE
            s_next = jnp.where(above, _dot_nt(q, k_next) * ATTN_SCALE, NEG_INF) + pad_next
            m = jnp.max(jnp.maximum(jnp.maximum(s_prev, s_cur), s_next), axis=-1, keepdims=True)
            m = jnp.maximum(m, sink)
            p_prev, p_cur, p_next = jnp.exp(s_prev - m), jnp.exp(s_cur - m), jnp.exp(s_next - m)
            denom = jnp.sum(p_prev + p_cur + p_next, axis=-1, keepdims=True) + jnp.exp(sink - m)
            o = (_dot(p_prev.astype(BF16), v_prev) + _dot(p_cur.astype(BF16), vc_ref[rs, hs])
                 + _dot(p_next.astype(BF16), v_next)) / denom
            for u in range(A_GROUP):
                h = A_GROUP * g + u
                o_ref[rs, h * HEAD_DIM:(h + 1) * HEAD_DIM] = (
                    o[u * WINDOW:(u + 1) * WINDOW].astype(o_ref.dtype))


def _window_attention(q, k, v, sink, batch, seq_len):
    n = q.shape[0]
    chunks = seq_len // _WIN_CHUNK
    blocks = seq_len // WINDOW

    def cur(b, c):
        return (b * chunks + c, 0)

    def prev(b, c):
        return (b * blocks + jnp.maximum(c * _WIN_SUB - 1, 0), 0)

    def nxt(b, c):
        return (b * blocks + jnp.minimum((c + 1) * _WIN_SUB, blocks - 1), 0)

    def kv_specs():
        return [pl.BlockSpec((WINDOW, A_KV_W), prev), pl.BlockSpec((_WIN_CHUNK, A_KV_W), cur),
                pl.BlockSpec((WINDOW, A_KV_W), nxt)]

    return pl.pallas_call(
        _win_attn_kernel,
        grid=(batch, chunks),
        in_specs=[pl.BlockSpec(memory_space=pltpu.SMEM), pl.BlockSpec((_WIN_CHUNK, A_Q_W), cur),
                  *kv_specs(), *kv_specs()],
        out_specs=pl.BlockSpec((_WIN_CHUNK, A_Q_W), cur),
        out_shape=jax.ShapeDtypeStruct((n, A_Q_W), BF16),
        compiler_params=_params("parallel", "parallel"),
        name="window_attn",
    )(sink, q, k, k, k, v, v, v)


_NA_KEYS = NA_ROWS * GRID_W
_NA_CHUNK = NA_ROWS * GRID_W
_NA_HALO = _NA_CHUNK // 2


def _na_bias_table(rpb):
    c = jnp.arange(GRID_W)
    cs = jnp.clip(c - NA_COLS // 2, 0, GRID_W - NA_COLS)
    col_ok = (c[None, :] >= cs[:, None]) & (c[None, :] < cs[:, None] + NA_COLS)
    dc = jnp.clip(c[None, :] - c[:, None], -(NA_COLS - 1), NA_COLS - 1) + NA_COLS - 1
    onehot = dc[:, :, None] == jnp.arange(2 * NA_COLS - 1)[None, None, :]
    by_col = jnp.sum(jnp.where(onehot[None, None], rpb.astype(F32)[:, :, None, None, :], 0.0), axis=-1)
    by_col = jnp.where(col_ok[None, None], by_col, NEG_INF)
    pats = [by_col[:, NA_ROWS - 1 - p:2 * NA_ROWS - 1 - p] for p in range(NA_ROWS)]
    bias = jnp.stack(pats, axis=0)
    return bias.transpose(0, 1, 3, 2, 4).reshape(NA_ROWS, B_HEADS, GRID_W, _NA_KEYS)


def _na_kernel(bias_ref, q_ref, kp_ref, kc_ref, kn_ref, vp_ref, vc_ref, vn_ref, o_ref,
               kbuf, vbuf, *, grid_rows):
    c = pl.program_id(1)
    kbuf[0:_NA_HALO] = kp_ref[...]
    kbuf[_NA_HALO:_NA_HALO + _NA_CHUNK] = kc_ref[...]
    kbuf[_NA_HALO + _NA_CHUNK:] = kn_ref[...]
    vbuf[0:_NA_HALO] = vp_ref[...]
    vbuf[_NA_HALO:_NA_HALO + _NA_CHUNK] = vc_ref[...]
    vbuf[_NA_HALO + _NA_CHUNK:] = vn_ref[...]
    halo_rows = _NA_HALO // GRID_W
    for i in range(NA_ROWS):
        r = c * NA_ROWS + i
        rs = jnp.clip(r - NA_ROWS // 2, 0, grid_rows - NA_ROWS)
        pat = r - rs
        start = pl.multiple_of((rs - (c * NA_ROWS - halo_rows)) * GRID_W, GRID_W)
        qs = slice(i * GRID_W, (i + 1) * GRID_W)
        for h in range(B_HEADS):
            hs = slice(h * HEAD_DIM, (h + 1) * HEAD_DIM)
            kw = kbuf[pl.ds(start, _NA_KEYS), hs]
            vw = vbuf[pl.ds(start, _NA_KEYS), hs]
            s = _dot_nt(q_ref[qs, hs], kw) * ATTN_SCALE + bias_ref[pat, h]
            m = jnp.max(s, axis=-1, keepdims=True)
            p = jnp.exp(s - m)
            denom = jnp.sum(p, axis=-1, keepdims=True)
            o_ref[qs, hs] = (_dot(p.astype(BF16), vw) / denom).astype(o_ref.dtype)


def _neighbourhood_attention(q, k, v, bias, batch, seq_len):
    n = q.shape[0]
    grid_rows = seq_len // GRID_W
    assert grid_rows >= 2 * NA_ROWS and seq_len % _NA_CHUNK == 0
    chunks = seq_len // _NA_CHUNK
    halos = seq_len // _NA_HALO

    def cur(b, c):
        return (b * chunks + c, 0)

    def prev(b, c):
        return (b * halos + jnp.maximum(2 * c - 1, 0), 0)

    def nxt(b, c):
        return (b * halos + jnp.minimum(2 * c + 2, halos - 1), 0)

    def kv_specs():
        return [pl.BlockSpec((_NA_HALO, B_W), prev), pl.BlockSpec((_NA_CHUNK, B_W), cur),
                pl.BlockSpec((_NA_HALO, B_W), nxt)]

    buf = pltpu.VMEM((_NA_CHUNK + 2 * _NA_HALO, B_W), BF16)
    return pl.pallas_call(
        functools.partial(_na_kernel, grid_rows=grid_rows),
        grid=(batch, chunks),
        in_specs=[_resident((NA_ROWS, B_HEADS, GRID_W, _NA_KEYS)),
                  pl.BlockSpec((_NA_CHUNK, B_W), cur), *kv_specs(), *kv_specs()],
        out_specs=pl.BlockSpec((_NA_CHUNK, B_W), cur),
        out_shape=jax.ShapeDtypeStruct((n, B_W), BF16),
        scratch_shapes=[buf, buf],
        compiler_params=_params("parallel", "parallel"),
        name="neighbourhood_attn",
    )(bias, q, k, k, k, v, v, v)


def _mem_kv_kernel(mem_ref, w_ref, k_ref, v_ref):
    kv = _dot(mem_ref[...].astype(BF16), w_ref[...])
    k_ref[...] = kv[:, :M_W].astype(k_ref.dtype)
    v_ref[...] = kv[:, M_W:].astype(v_ref.dtype)


def _memory_kv(mem2, w_mem_kv, tm=256):
    n = mem2.shape[0]
    out = jax.ShapeDtypeStruct((n, M_W), BF16)
    return pl.pallas_call(
        _mem_kv_kernel,
        grid=(n // tm,),
        in_specs=[pl.BlockSpec((tm, D_MODEL), lambda i: (i, 0)), _resident((D_MODEL, 2 * M_W))],
        out_specs=[pl.BlockSpec((tm, M_W), lambda i: (i, 0))] * 2,
        out_shape=[out, out],
        compiler_params=_params("parallel"),
        name="memory_kv",
    )(mem2, w_mem_kv)


def _mem_attn_kernel(q_ref, k_ref, v_ref, o_ref):
    for h in range(M_HEADS):
        hs = slice(h * HEAD_DIM, (h + 1) * HEAD_DIM)
        s = _dot_nt(q_ref[:, hs], k_ref[:, hs]) * ATTN_SCALE
        m = jnp.max(s, axis=-1, keepdims=True)
        p = jnp.exp(s - m)
        denom = jnp.sum(p, axis=-1, keepdims=True)
        o_ref[:, hs] = (_dot(p.astype(BF16), v_ref[:, hs]) / denom).astype(o_ref.dtype)


def _memory_attention(q, km, vm, batch, seq_len, tm=512):
    n = q.shape[0]
    n_mem = km.shape[0] // batch
    chunks = seq_len // tm
    return pl.pallas_call(
        _mem_attn_kernel,
        grid=(batch, chunks),
        in_specs=[pl.BlockSpec((tm, M_W), lambda b, c: (b * chunks + c, 0)),
                  pl.BlockSpec((n_mem, M_W), lambda b, c: (b, 0)),
                  pl.BlockSpec((n_mem, M_W), lambda b, c: (b, 0))],
        out_specs=pl.BlockSpec((tm, M_W), lambda b, c: (b * chunks + c, 0)),
        out_shape=jax.ShapeDtypeStruct((n, M_W), BF16),
        compiler_params=_params("parallel", "parallel"),
        name="memory_attn",
    )(q, km, vm)


_MERGE_COLS = 512


def _merge_kernel(x_ref, oa_ref, ob_ref, om_ref, wg0_ref, wg1_ref, wg2_ref, bg0_ref, bg1_ref,
                  bg2_ref, wa_ref, wb_ref, wm_ref, wo_ref, g_ref, b_ref, y_ref, yb_ref,
                  xb_ref, acc_ref, *, alpha):
    j = pl.program_id(1)

    @pl.when(j == 0)
    def _():
        xb_ref[...] = x_ref[...].astype(BF16)
        acc_ref[...] = jnp.zeros_like(acc_ref)

    xb = xb_ref[...]
    merged = (jax.nn.sigmoid(_dot(xb, wg0_ref[...]) + bg0_ref[...]) * _dot(oa_ref[...], wa_ref[...])
              + jax.nn.sigmoid(_dot(xb, wg1_ref[...]) + bg1_ref[...]) * _dot(ob_ref[...], wb_ref[...])
              + jax.nn.sigmoid(_dot(xb, wg2_ref[...]) + bg2_ref[...]) * _dot(om_ref[...], wm_ref[...]))
    acc_ref[...] += _dot(merged.astype(BF16), wo_ref[...])

    @pl.when(j == pl.num_programs(1) - 1)
    def _():
        y = _layer_norm(alpha * x_ref[...] + acc_ref[...], g_ref[...], b_ref[...])
        y_ref[...] = y
        yb_ref[...] = y.astype(BF16)


def _merge(x2, o_a, o_b, o_m, w_gate, b_gate, w_pa, w_pb, w_pm, w_out, ln_g, ln_b, alpha, tm=512):
    n = x2.shape[0]
    steps = D_MODEL // _MERGE_COLS
    row = lambda i, j: (i, 0)

    def gate_w(branch):
        return pl.BlockSpec((D_MODEL, _MERGE_COLS), lambda i, j: (0, branch * steps + j))

    def gate_b(branch):
        return pl.BlockSpec((1, _MERGE_COLS), lambda i, j: (0, branch * steps + j))

    col = lambda i, j: (0, j)
    vec = pl.BlockSpec((1, D_MODEL), lambda i, j: (0, 0))

    def once(shape):
        return pl.BlockSpec(shape, row, pipeline_mode=pl.Buffered(1))

    return pl.pallas_call(
        functools.partial(_merge_kernel, alpha=alpha),
        grid=(n // tm, steps),
        in_specs=[once((tm, D_MODEL)), once((tm, A_Q_W)), once((tm, B_W)), once((tm, M_W)),
                  gate_w(0), gate_w(1), gate_w(2), gate_b(0), gate_b(1), gate_b(2),
                  pl.BlockSpec((A_Q_W, _MERGE_COLS), col), pl.BlockSpec((B_W, _MERGE_COLS), col),
                  pl.BlockSpec((M_W, _MERGE_COLS), col),
                  pl.BlockSpec((_MERGE_COLS, D_MODEL), lambda i, j: (j, 0)), vec, vec],
        out_specs=[pl.BlockSpec((tm, D_MODEL), row), pl.BlockSpec((tm, D_MODEL), row)],
        out_shape=[jax.ShapeDtypeStruct((n, D_MODEL), F32), jax.ShapeDtypeStruct((n, D_MODEL), BF16)],
        scratch_shapes=[pltpu.VMEM((tm, D_MODEL), BF16), pltpu.VMEM((tm, D_MODEL), F32)],
        compiler_params=_params("parallel", "arbitrary"),
        name="merge_out_ln",
    )(x2, o_a, o_b, o_m, w_gate, w_gate, w_gate, b_gate, b_gate, b_gate, w_pa, w_pb, w_pm, w_out,
      ln_g, ln_b)


def _take_top(s, count, row_id):
    order = jnp.full(s.shape, float(count), F32)
    vals, ids = [], []
    for a in range(count):
        m = jnp.max(s, axis=0, keepdims=True)
        rid = jnp.min(jnp.where(s == m, row_id, _NO_ROW), axis=0, keepdims=True)
        hit = row_id == rid
        order = jnp.where(hit, float(a), order)
        s = jnp.where(hit, -jnp.inf, s)
        vals.append(m)
        ids.append(rid)
    return vals, ids, order


_NO_ROW = 1e9
_CAND_COUNTS = tuple(PEER_TOPK // (a + 1) for a in range(PEER_TOPK))
_CAND_ROWS = -(-sum(_CAND_COUNTS) // SUBLANES) * SUBLANES


def _candidate_ids():
    ids = [a * PEER_TOPK + b for a, nb in enumerate(_CAND_COUNTS) for b in range(nb)]
    ids += [_NO_ROW] * (_CAND_ROWS - len(ids))
    return jnp.broadcast_to(jnp.asarray(ids, F32)[:, None], (_CAND_ROWS, LANES))


def _peer_prep_kernel(xb_ref, wq_ref, k1_ref, k2_ref, cid_ref, n1_ref, e1_ref, r2_ref, e2_ref,
                      st_ref):
    tm = xb_ref.shape[0]
    q = _dot(xb_ref[...], wq_ref[...]).astype(BF16)
    for h in range(PEER_HEADS):
        q1 = q[:, h * PEER_QDIM:h * PEER_QDIM + PEER_HALF]
        q2 = q[:, h * PEER_QDIM + PEER_HALF:(h + 1) * PEER_QDIM]
        st_ref[2 * h] = _dot_nt(k1_ref[...], q1)
        st_ref[2 * h + 1] = _dot_nt(k2_ref[...], q2)
    lane_chunks = tm // LANES

    def body(it, carry):
        h = it // lane_chunks
        lanes = pl.ds(pl.multiple_of((it % lane_chunks) * LANES, LANES), LANES)
        s1 = st_ref[2 * h, :, lanes]
        s2 = st_ref[2 * h + 1, :, lanes]
        key_id = lax.broadcasted_iota(jnp.int32, (N_KEYS, LANES), 0).astype(F32)
        t1, _, order1 = _take_top(s1, PEER_TOPK, key_id)
        t2, _, order2 = _take_top(s2, PEER_TOPK, key_id)
        t2_all = jnp.concatenate(t2, axis=0)
        pad = jnp.full((_CAND_ROWS - sum(_CAND_COUNTS), LANES), -jnp.inf, F32)
        cand = jnp.concatenate(
            [t1[a] + t2_all[:nb] for a, nb in enumerate(_CAND_COUNTS)] + [pad], axis=0)
        top, cid, _ = _take_top(cand, PEER_TOPK, cid_ref[...])
        a_id = lax.broadcasted_iota(jnp.int32, (PEER_TOPK, LANES), 0).astype(F32)
        taken = jnp.zeros((PEER_TOPK, LANES), F32)
        z = jnp.zeros((1, LANES), F32)
        for k in range(PEER_TOPK):
            taken = taken + jnp.where(a_id == jnp.floor(cid[k] * (1.0 / PEER_TOPK)), 1.0, 0.0)
            z = z + jnp.exp(top[k] - top[0])
        n1 = jnp.zeros((N_KEYS, LANES), F32)
        for a in range(PEER_TOPK):
            n1 = jnp.where(order1 == float(a), taken[a:a + 1, :], n1)
        keys = pl.ds(pl.multiple_of(h * N_KEYS, N_KEYS), N_KEYS)
        n1_ref[keys, lanes] = n1
        e1_ref[keys, lanes] = jnp.exp(s1 - t1[0])
        r2_ref[keys, lanes] = order2.astype(r2_ref.dtype)
        e2_ref[keys, lanes] = (jnp.exp(s2 - t2[0]) / z).astype(e2_ref.dtype)
        return carry

    lax.fori_loop(0, PEER_HEADS * lane_chunks, body, 0)


def _peer_prep(x1b, w_q, keys1, keys2, tm=512):
    n = x1b.shape[0]
    by_key1 = jax.ShapeDtypeStruct((PEER_HEADS * N_KEYS, n), F32)
    by_key2 = jax.ShapeDtypeStruct((PEER_HEADS * N_KEYS, n), BF16)
    spec = pl.BlockSpec((PEER_HEADS * N_KEYS, tm), lambda i: (0, i))
    return pl.pallas_call(
        _peer_prep_kernel,
        grid=(n // tm,),
        in_specs=[pl.BlockSpec((tm, D_MODEL), lambda i: (i, 0)),
                  _resident((D_MODEL, PEER_HEADS * PEER_QDIM)),
                  _resident((N_KEYS, PEER_HALF)), _resident((N_KEYS, PEER_HALF)),
                  _resident((_CAND_ROWS, LANES))],
        out_specs=[spec] * 4,
        out_shape=[by_key1, by_key1, by_key2, by_key2],
        scratch_shapes=[pltpu.VMEM((2 * PEER_HEADS, N_KEYS, tm), F32)],
        compiler_params=_params("parallel"),
        name="peer_retrieve",
    )(x1b, w_q, keys1, keys2, _candidate_ids())


def _peer_kernel(xb_ref, u_ref, vt_ref, n1_ref, e1_ref, r2_ref, e2_ref, x_ref, g_ref, b_ref,
                 y_ref, acc_ref, a_ref, w_ref, *, alpha):
    j = pl.program_id(1)
    te, tm = a_ref.shape

    @pl.when(j == 0)
    def _():
        acc_ref[...] = jnp.zeros_like(acc_ref)

    a_ref[...] = _dot_nt(u_ref[...], xb_ref[...])
    keys_per_step = te // N_KEYS
    first_key = pl.multiple_of(j * keys_per_step, SUBLANES)

    def first_key_row(ref, h, sub, ls):
        return ref[pl.ds(h * N_KEYS + first_key, SUBLANES), ls][sub:sub + 1]

    chunk = 2 * LANES
    for sub in range(keys_per_step):
        es = slice(sub * N_KEYS, (sub + 1) * N_KEYS)
        for tc in range(tm // chunk):
            ls = slice(tc * chunk, (tc + 1) * chunk)
            gate = jnp.zeros((N_KEYS, chunk), BF16)
            for h in range(PEER_HEADS):
                ks = slice(h * N_KEYS, (h + 1) * N_KEYS)
                n1 = jnp.broadcast_to(first_key_row(n1_ref, h, sub, ls).astype(BF16), gate.shape)
                e1 = jnp.broadcast_to(first_key_row(e1_ref, h, sub, ls).astype(BF16), gate.shape)
                gate = gate + jnp.where(r2_ref[ks, ls] < n1, e2_ref[ks, ls] * e1,
                                        jnp.zeros_like(gate))
            a = a_ref[es, ls]
            gelu = 0.5 * a * (1.0 + lax.erf(a * (2.0 ** -0.5)))
            w_ref[es, ls] = gelu.astype(BF16) * gate

    acc_ref[...] += _dot(vt_ref[...], w_ref[...])

    @pl.when(j == pl.num_programs(1) - 1)
    def _():
        ff = acc_ref[...].T
        y_ref[...] = _layer_norm(alpha * x_ref[...] + ff, g_ref[...], b_ref[...])


def _peer(x1, x1b, u, vt, n1, e1, r2, e2, ln_g, ln_b, alpha, tm=512, te=1024):
    n = x1.shape[0]
    n_experts = u.shape[0]
    assert (te // N_KEYS) % SUBLANES == 0
    row = lambda i, j: (i, 0)
    sel = pl.BlockSpec((PEER_HEADS * N_KEYS, tm), lambda i, j: (0, i),
                       pipeline_mode=pl.Buffered(1))
    vec = pl.BlockSpec((1, D_MODEL), lambda i, j: (0, 0))
    return pl.pallas_call(
        functools.partial(_peer_kernel, alpha=alpha),
        grid=(n // tm, n_experts // te),
        in_specs=[pl.BlockSpec((tm, D_MODEL), row),
                  pl.BlockSpec((te, D_MODEL), lambda i, j: (j, 0)),
                  pl.BlockSpec((D_MODEL, te), lambda i, j: (0, j)),
                  sel, sel, sel, sel,
                  pl.BlockSpec((tm, D_MODEL), row, pipeline_mode=pl.Buffered(1)), vec, vec],
        out_specs=pl.BlockSpec((tm, D_MODEL), row),
        out_shape=jax.ShapeDtypeStruct((n, D_MODEL), F32),
        scratch_shapes=[pltpu.VMEM((D_MODEL, tm), F32), pltpu.VMEM((te, tm), F32),
                        pltpu.VMEM((te, tm), BF16)],
        compiler_params=_params("parallel", "arbitrary"),
        name="peer_experts_ln",
    )(x1b, u, vt, n1, e1, r2, e2, x1, ln_g, ln_b)


def _encoder_layer(x, mem, p, alpha):
    batch, seq_len, _ = x.shape
    x2 = x.reshape(batch * seq_len, D_MODEL)
    q_a, k_a, v_a, q_b, k_b, v_b, q_m = _project(x2, p["w_qkv"], _rope_tables(seq_len), seq_len)
    o_a = _window_attention(q_a, k_a, v_a, p["a_sink"], batch, seq_len)
    o_b = _neighbourhood_attention(q_b, k_b, v_b, p["na_bias"], batch, seq_len)
    km, vm = _memory_kv(mem.reshape(-1, D_MODEL), p["w_mem_kv"])
    o_m = _memory_attention(q_m, km, vm, batch, seq_len)
    x1, x1b = _merge(x2, o_a, o_b, o_m, p["w_gate"], p["b_gate"], p["w_proj_a"], p["w_proj_b"],
                     p["w_proj_m"], p["w_out"], p["ln1_g"], p["ln1_b"], alpha)
    n1, e1, r2, e2 = _peer_prep(x1b, p["w_peer_q"], p["peer_keys1"], p["peer_keys2"])
    y = _peer(x1, x1b, p["peer_u"], p["peer_vt"], n1, e1, r2, e2, p["ln2_g"], p["ln2_b"], alpha)
    return y.reshape(batch, seq_len, D_MODEL)


def _layer_params(w_in, b_gate, a_sink, na_rpb, w_mem_kv, w_proj_a, w_proj_b, w_proj_m, w_out,
                  ln1_g, ln1_b, w_peer_q, peer_keys1, peer_keys2, peer_u, peer_v, ln2_g, ln2_b):
    row = lambda v: v.reshape(1, -1).astype(F32)
    return {
        "w_qkv": w_in[:, :QKV_W].astype(BF16),
        "w_gate": w_in[:, QKV_W:].astype(BF16),
        "b_gate": row(b_gate),
        "a_sink": a_sink.astype(F32),
        "na_bias": _na_bias_table(na_rpb),
        "w_mem_kv": w_mem_kv.astype(BF16),
        "w_proj_a": w_proj_a.astype(BF16),
        "w_proj_b": w_proj_b.astype(BF16),
        "w_proj_m": w_proj_m.astype(BF16),
        "w_out": w_out.astype(BF16),
        "ln1_g": row(ln1_g),
        "ln1_b": row(ln1_b),
        "w_peer_q": w_peer_q.astype(BF16),
        "peer_keys1": peer_keys1.astype(BF16),
        "peer_keys2": peer_keys2.astype(BF16),
        "peer_u": peer_u.astype(BF16),
        "peer_vt": peer_v.T.astype(BF16),
        "ln2_g": row(ln2_g),
        "ln2_b": row(ln2_b),
    }


def kernel(x_prompt, x_sample, mem_prompt, mem_sample, w_in, b_gate, a_sink, na_rpb, w_mem_kv,
           w_proj_a, w_proj_b, w_proj_m, w_out, ln1_g, ln1_b, w_peer_q, peer_keys1, peer_keys2,
           peer_u, peer_v, ln2_g, ln2_b):
    depth = w_in.shape[0]
    alpha = (2.0 * depth) ** 0.25
    y_prompt, y_sample = x_prompt, x_sample
    for l in range(depth):
        p = _layer_params(w_in[l], b_gate[l], a_sink[l], na_rpb[l], w_mem_kv[l], w_proj_a[l],
                          w_proj_b[l], w_proj_m[l], w_out[l], ln1_g[l], ln1_b[l], w_peer_q[l],
                          peer_keys1[l], peer_keys2[l], peer_u[l], peer_v[l], ln2_g[l], ln2_b[l])
        y_prompt = _encoder_layer(y_prompt, mem_prompt, p, alpha)
        y_sample = _encoder_layer(y_sample, mem_sample, p, alpha)
    return (y_prompt, y_sample)
```

```python
import functools

import jax
import jax.numpy as jnp
from jax import lax
from jax.experimental import pallas as pl
from jax.experimental.pallas import tpu as pltpu

D_MODEL = 2048
HEAD_DIM = 128
A_HEADS = 8
A_KV_HEADS = 2
A_GROUP = A_HEADS // A_KV_HEADS
WINDOW = 128
ROPE_THETA = 500000.0
ROPE_DIM = HEAD_DIM // 4
B_HEADS = 4
GRID_W = 64
NA_ROWS = 8
NA_COLS = 16
M_HEADS = 4
N_BRANCH = 3
A_Q_W = A_HEADS * HEAD_DIM
A_KV_W = A_KV_HEADS * HEAD_DIM
B_W = B_HEADS * HEAD_DIM
M_W = M_HEADS * HEAD_DIM
QKV_W = A_Q_W + 2 * A_KV_W + 3 * B_W + M_W
PEER_HEADS = 8
PEER_QDIM = 256
PEER_HALF = PEER_QDIM // 2
N_KEYS = 128
PEER_TOPK = 16
LN_EPS = 1e-5
NEG_INF = -1e30
ATTN_SCALE = HEAD_DIM ** -0.5

LANES = 128
SUBLANES = 8
VMEM_LIMIT = 56 * 1024 * 1024

BF16 = jnp.bfloat16
F32 = jnp.float32

_PROJ_PIECES = (("q_a", A_Q_W, True), ("k_a", A_KV_W, True), ("v_a", A_KV_W, False),
                ("q_b", B_W, False), ("k_b", B_W, False), ("v_b", B_W, False), ("q_m", M_W, False))
_COL_CHUNK = 512


def _dot(a, b):
    return jnp.dot(a, b, preferred_element_type=F32)


def _dot_nt(a, b):
    return lax.dot_general(a, b, (((1,), (1,)), ((), ())), preferred_element_type=F32)


def _params(*semantics):
    return pltpu.CompilerParams(dimension_semantics=semantics, vmem_limit_bytes=VMEM_LIMIT)


def _resident(shape):
    return pl.BlockSpec(shape, lambda *_: (0,) * len(shape), pipeline_mode=pl.Buffered(1))


def _layer_norm(h, g, b):
    mu = jnp.mean(h, axis=-1, keepdims=True)
    hc = h - mu
    var = jnp.mean(hc * hc, axis=-1, keepdims=True)
    return hc * lax.rsqrt(var + LN_EPS) * g + b


def _proj_kernel(x_ref, w_ref, cos_ref, sin_up_ref, sin_dn_ref, *out_refs):
    xb = x_ref[...].astype(BF16)
    off = 0
    for (_, width, rotary), o_ref in zip(_PROJ_PIECES, out_refs):
        for c0 in range(0, width, _COL_CHUNK):
            cw = min(_COL_CHUNK, width - c0)
            acc = _dot(xb, w_ref[:, off + c0:off + c0 + cw])
            if rotary:
                reps = cw // HEAD_DIM
                cos = jnp.concatenate([cos_ref[...]] * reps, axis=1)
                sin_up = jnp.concatenate([sin_up_ref[...]] * reps, axis=1)
                sin_dn = jnp.concatenate([sin_dn_ref[...]] * reps, axis=1)
                half = ROPE_DIM // 2
                acc = (acc * cos + pltpu.roll(acc, half, 1) * sin_up
                       + pltpu.roll(acc, cw - half, 1) * sin_dn)
            o_ref[:, c0:c0 + cw] = acc.astype(o_ref.dtype)
        off += width


def _rope_tables(seq_len):
    half = ROPE_DIM // 2
    inv_freq = ROPE_THETA ** (-jnp.arange(half, dtype=F32) * 2.0 / ROPE_DIM)
    ang = jnp.arange(seq_len, dtype=F32)[:, None] * inv_freq[None, :]
    cos, sin = jnp.cos(ang), jnp.sin(ang)
    zeros = jnp.zeros((seq_len, HEAD_DIM - ROPE_DIM), F32)
    zh = jnp.zeros((seq_len, half), F32)
    cos_t = jnp.concatenate([cos, cos, zeros + 1.0], axis=1)
    sin_up = jnp.concatenate([zh, sin, zeros], axis=1)
    sin_dn = jnp.concatenate([-sin, zh, zeros], axis=1)
    return cos_t, sin_up, sin_dn


def _project(x2, w_qkv, rope, seq_len, tm=512):
    n = x2.shape[0]
    blocks_per_seq = seq_len // tm
    tab_spec = pl.BlockSpec((tm, HEAD_DIM), lambda i: (i % blocks_per_seq, 0))
    return pl.pallas_call(
        _proj_kernel,
        grid=(n // tm,),
        in_specs=[pl.BlockSpec((tm, D_MODEL), lambda i: (i, 0)),
                  _resident((D_MODEL, QKV_W)), tab_spec, tab_spec, tab_spec],
        out_specs=[pl.BlockSpec((tm, w), lambda i: (i, 0)) for _, w, _ in _PROJ_PIECES],
        out_shape=[jax.ShapeDtypeStruct((n, w), BF16) for _, w, _ in _PROJ_PIECES],
        compiler_params=_params("parallel"),
        name="qkv_proj",
    )(x2, w_qkv, *rope)


_WIN_CHUNK = 512
_WIN_SUB = _WIN_CHUNK // WINDOW


def _win_attn_kernel(sink_ref, q_ref, kp_ref, kc_ref, kn_ref, vp_ref, vc_ref, vn_ref, o_ref):
    c = pl.program_id(1)
    rows = A_GROUP * WINDOW
    qi = lax.broadcasted_iota(jnp.int32, (rows, WINDOW), 0) % WINDOW
    kj = lax.broadcasted_iota(jnp.int32, (rows, WINDOW), 1)
    below = kj >= qi
    above = kj <= qi
    prev_pad = jnp.where(c > 0, 0.0, NEG_INF)
    next_pad = jnp.where(c < pl.num_programs(1) - 1, 0.0, NEG_INF)
    for g in range(A_KV_HEADS):
        hs = slice(g * HEAD_DIM, (g + 1) * HEAD_DIM)
        sink = jnp.concatenate(
            [jnp.full((WINDOW, 1), sink_ref[A_GROUP * g + u], F32) for u in range(A_GROUP)], axis=0)
        for sb in range(_WIN_SUB):
            rs = slice(sb * WINDOW, (sb + 1) * WINDOW)
            q = jnp.concatenate(
                [q_ref[rs, (A_GROUP * g + u) * HEAD_DIM:(A_GROUP * g + u + 1) * HEAD_DIM]
                 for u in range(A_GROUP)], axis=0)
            if sb == 0:
                k_prev, v_prev, pad_prev = kp_ref[:, hs], vp_ref[:, hs], prev_pad
            else:
                ps = slice((sb - 1) * WINDOW, sb * WINDOW)
                k_prev, v_prev, pad_prev = kc_ref[ps, hs], vc_ref[ps, hs], 0.0
            if sb == _WIN_SUB - 1:
                k_next, v_next, pad_next = kn_ref[:, hs], vn_ref[:, hs], next_pad
            else:
                ns = slice((sb + 1) * WINDOW, (sb + 2) * WINDOW)
                k_next, v_next, pad_next = kc_ref[ns, hs], vc_ref[ns, hs], 0.0
            s_prev = jnp.where(below, _dot_nt(q, k_prev) * ATTN_SCALE, NEG_INF) + pad_prev
            s_cur = _dot_nt(q, kc_ref[rs, hs]) * ATTN_SCALE
            s_next = jnp.where(above, _dot_nt(q, k_next) * ATTN_SCALE, NEG_INF) + pad_next
            m = jnp.max(jnp.maximum(jnp.maximum(s_prev, s_cur), s_next), axis=-1, keepdims=True)
            m = jnp.maximum(m, sink)
            p_prev, p_cur, p_next = jnp.exp(s_prev - m), jnp.exp(s_cur - m), jnp.exp(s_next - m)
            denom = jnp.sum(p_prev + p_cur + p_next, axis=-1, keepdims=True) + jnp.exp(sink - m)
            o = (_dot(p_prev.astype(BF16), v_prev) + _dot(p_cur.astype(BF16), vc_ref[rs, hs])
                 + _dot(p_next.astype(BF16), v_next)) / denom
            for u in range(A_GROUP):
                h = A_GROUP * g + u
                o_ref[rs, h * HEAD_DIM:(h + 1) * HEAD_DIM] = (
                    o[u * WINDOW:(u + 1) * WINDOW].astype(o_ref.dtype))


def _window_attention(q, k, v, sink, batch, seq_len):
    n = q.shape[0]
    chunks = seq_len // _WIN_CHUNK
    blocks = seq_len // WINDOW

    def cur(b, c):
        return (b * chunks + c, 0)

    def prev(b, c):
        return (b * blocks + jnp.maximum(c * _WIN_SUB - 1, 0), 0)

    def nxt(b, c):
        return (b * blocks + jnp.minimum((c + 1) * _WIN_SUB, blocks - 1), 0)

    def kv_specs():
        return [pl.BlockSpec((WINDOW, A_KV_W), prev), pl.BlockSpec((_WIN_CHUNK, A_KV_W), cur),
                pl.BlockSpec((WINDOW, A_KV_W), nxt)]

    return pl.pallas_call(
        _win_attn_kernel,
        grid=(batch, chunks),
        in_specs=[pl.BlockSpec(memory_space=pltpu.SMEM), pl.BlockSpec((_WIN_CHUNK, A_Q_W), cur),
                  *kv_specs(), *kv_specs()],
        out_specs=pl.BlockSpec((_WIN_CHUNK, A_Q_W), cur),
        out_shape=jax.ShapeDtypeStruct((n, A_Q_W), BF16),
        compiler_params=_params("parallel", "parallel"),
        name="window_attn",
    )(sink, q, k, k, k, v, v, v)


_NA_KEYS = NA_ROWS * GRID_W
_NA_CHUNK = NA_ROWS * GRID_W
_NA_HALO = _NA_CHUNK // 2


def _na_bias_table(rpb):
    c = jnp.arange(GRID_W)
    cs = jnp.clip(c - NA_COLS // 2, 0, GRID_W - NA_COLS)
    col_ok = (c[None, :] >= cs[:, None]) & (c[None, :] < cs[:, None] + NA_COLS)
    dc = jnp.clip(c[None, :] - c[:, None], -(NA_COLS - 1), NA_COLS - 1) + NA_COLS - 1
    onehot = dc[:, :, None] == jnp.arange(2 * NA_COLS - 1)[None, None, :]
    by_col = jnp.sum(jnp.where(onehot[None, None], rpb.astype(F32)[:, :, None, None, :], 0.0), axis=-1)
    by_col = jnp.where(col_ok[None, None], by_col, NEG_INF)
    pats = [by_col[:, NA_ROWS - 1 - p:2 * NA_ROWS - 1 - p] for p in range(NA_ROWS)]
    bias = jnp.stack(pats, axis=0)
    return bias.transpose(0, 1, 3, 2, 4).reshape(NA_ROWS, B_HEADS, GRID_W, _NA_KEYS)


def _na_kernel(bias_ref, q_ref, kp_ref, kc_ref, kn_ref, vp_ref, vc_ref, vn_ref, o_ref,
               kbuf, vbuf, *, grid_rows):
    c = pl.program_id(1)
    kbuf[0:_NA_HALO] = kp_ref[...]
    kbuf[_NA_HALO:_NA_HALO + _NA_CHUNK] = kc_ref[...]
    kbuf[_NA_HALO + _NA_CHUNK:] = kn_ref[...]
    vbuf[0:_NA_HALO] = vp_ref[...]
    vbuf[_NA_HALO:_NA_HALO + _NA_CHUNK] = vc_ref[...]
    vbuf[_NA_HALO + _NA_CHUNK:] = vn_ref[...]
    halo_rows = _NA_HALO // GRID_W
    for i in range(NA_ROWS):
        r = c * NA_ROWS + i
        rs = jnp.clip(r - NA_ROWS // 2, 0, grid_rows - NA_ROWS)
        pat = r - rs
        start = pl.multiple_of((rs - (c * NA_ROWS - halo_rows)) * GRID_W, GRID_W)
        qs = slice(i * GRID_W, (i + 1) * GRID_W)
        for h in range(B_HEADS):
            hs = slice(h * HEAD_DIM, (h + 1) * HEAD_DIM)
            kw = kbuf[pl.ds(start, _NA_KEYS), hs]
            vw = vbuf[pl.ds(start, _NA_KEYS), hs]
            s = _dot_nt(q_ref[qs, hs], kw) * ATTN_SCALE + bias_ref[pat, h]
            m = jnp.max(s, axis=-1, keepdims=True)
            p = jnp.exp(s - m)
            denom = jnp.sum(p, axis=-1, keepdims=True)
            o_ref[qs, hs] = (_dot(p.astype(BF16), vw) / denom).astype(o_ref.dtype)


def _neighbourhood_attention(q, k, v, bias, batch, seq_len):
    n = q.shape[0]
    grid_rows = seq_len // GRID_W
    assert grid_rows >= 2 * NA_ROWS and seq_len % _NA_CHUNK == 0
    chunks = seq_len // _NA_CHUNK
    halos = seq_len // _NA_HALO

    def cur(b, c):
        return (b * chunks + c, 0)

    def prev(b, c):
        return (b * halos + jnp.maximum(2 * c - 1, 0), 0)

    def nxt(b, c):
        return (b * halos + jnp.minimum(2 * c + 2, halos - 1), 0)

    def kv_specs():
        return [pl.BlockSpec((_NA_HALO, B_W), prev), pl.BlockSpec((_NA_CHUNK, B_W), cur),
                pl.BlockSpec((_NA_HALO, B_W), nxt)]

    buf = pltpu.VMEM((_NA_CHUNK + 2 * _NA_HALO, B_W), BF16)
    return pl.pallas_call(
        functools.partial(_na_kernel, grid_rows=grid_rows),
        grid=(batch, chunks),
        in_specs=[_resident((NA_ROWS, B_HEADS, GRID_W, _NA_KEYS)),
                  pl.BlockSpec((_NA_CHUNK, B_W), cur), *kv_specs(), *kv_specs()],
        out_specs=pl.BlockSpec((_NA_CHUNK, B_W), cur),
        out_shape=jax.ShapeDtypeStruct((n, B_W), BF16),
        scratch_shapes=[buf, buf],
        compiler_params=_params("parallel", "parallel"),
        name="neighbourhood_attn",
    )(bias, q, k, k, k, v, v, v)


def _mem_kv_kernel(mem_ref, w_ref, k_ref, v_ref):
    kv = _dot(mem_ref[...].astype(BF16), w_ref[...])
    k_ref[...] = kv[:, :M_W].astype(k_ref.dtype)
    v_ref[...] = kv[:, M_W:].astype(v_ref.dtype)


def _memory_kv(mem2, w_mem_kv, tm=256):
    n = mem2.shape[0]
    out = jax.ShapeDtypeStruct((n, M_W), BF16)
    return pl.pallas_call(
        _mem_kv_kernel,
        grid=(n // tm,),
        in_specs=[pl.BlockSpec((tm, D_MODEL), lambda i: (i, 0)), _resident((D_MODEL, 2 * M_W))],
        out_specs=[pl.BlockSpec((tm, M_W), lambda i: (i, 0))] * 2,
        out_shape=[out, out],
        compiler_params=_params("parallel"),
        name="memory_kv",
    )(mem2, w_mem_kv)


def _mem_attn_kernel(q_ref, k_ref, v_ref, o_ref):
    for h in range(M_HEADS):
        hs = slice(h * HEAD_DIM, (h + 1) * HEAD_DIM)
        s = _dot_nt(q_ref[:, hs], k_ref[:, hs]) * ATTN_SCALE
        m = jnp.max(s, axis=-1, keepdims=True)
        p = jnp.exp(s - m)
        denom = jnp.sum(p, axis=-1, keepdims=True)
        o_ref[:, hs] = (_dot(p.astype(BF16), v_ref[:, hs]) / denom).astype(o_ref.dtype)


def _memory_attention(q, km, vm, batch, seq_len, tm=512):
    n = q.shape[0]
    n_mem = km.shape[0] // batch
    chunks = seq_len // tm
    return pl.pallas_call(
        _mem_attn_kernel,
        grid=(batch, chunks),
        in_specs=[pl.BlockSpec((tm, M_W), lambda b, c: (b * chunks + c, 0)),
                  pl.BlockSpec((n_mem, M_W), lambda b, c: (b, 0)),
                  pl.BlockSpec((n_mem, M_W), lambda b, c: (b, 0))],
        out_specs=pl.BlockSpec((tm, M_W), lambda b, c: (b * chunks + c, 0)),
        out_shape=jax.ShapeDtypeStruct((n, M_W), BF16),
        compiler_params=_params("parallel", "parallel"),
        name="memory_attn",
    )(q, km, vm)


_MERGE_COLS = 512


def _merge_kernel(x_ref, oa_ref, ob_ref, om_ref, wg0_ref, wg1_ref, wg2_ref, bg0_ref, bg1_ref,
                  bg2_ref, wa_ref, wb_ref, wm_ref, wo_ref, g_ref, b_ref, y_ref, yb_ref,
                  xb_ref, acc_ref, *, alpha):
    j = pl.program_id(1)

    @pl.when(j == 0)
    def _():
        xb_ref[...] = x_ref[...].astype(BF16)
        acc_ref[...] = jnp.zeros_like(acc_ref)

    xb = xb_ref[...]
    merged = (jax.nn.sigmoid(_dot(xb, wg0_ref[...]) + bg0_ref[...]) * _dot(oa_ref[...], wa_ref[...])
              + jax.nn.sigmoid(_dot(xb, wg1_ref[...]) + bg1_ref[...]) * _dot(ob_ref[...], wb_ref[...])
              + jax.nn.sigmoid(_dot(xb, wg2_ref[...]) + bg2_ref[...]) * _dot(om_ref[...], wm_ref[...]))
    acc_ref[...] += _dot(merged.astype(BF16), wo_ref[...])

    @pl.when(j == pl.num_programs(1) - 1)
    def _():
        y = _layer_norm(alpha * x_ref[...] + acc_ref[...], g_ref[...], b_ref[...])
        y_ref[...] = y
        yb_ref[...] = y.astype(BF16)


def _merge(x2, o_a, o_b, o_m, w_gate, b_gate, w_pa, w_pb, w_pm, w_out, ln_g, ln_b, alpha, tm=512):
    n = x2.shape[0]
    steps = D_MODEL // _MERGE_COLS
    row = lambda i, j: (i, 0)

    def gate_w(branch):
        return pl.BlockSpec((D_MODEL, _MERGE_COLS), lambda i, j: (0, branch * steps + j))

    def gate_b(branch):
        return pl.BlockSpec((1, _MERGE_COLS), lambda i, j: (0, branch * steps + j))

    col = lambda i, j: (0, j)
    vec = pl.BlockSpec((1, D_MODEL), lambda i, j: (0, 0))

    def once(shape):
        return pl.BlockSpec(shape, row, pipeline_mode=pl.Buffered(1))

    return pl.pallas_call(
        functools.partial(_merge_kernel, alpha=alpha),
        grid=(n // tm, steps),
        in_specs=[once((tm, D_MODEL)), once((tm, A_Q_W)), once((tm, B_W)), once((tm, M_W)),
                  gate_w(0), gate_w(1), gate_w(2), gate_b(0), gate_b(1), gate_b(2),
                  pl.BlockSpec((A_Q_W, _MERGE_COLS), col), pl.BlockSpec((B_W, _MERGE_COLS), col),
                  pl.BlockSpec((M_W, _MERGE_COLS), col),
                  pl.BlockSpec((_MERGE_COLS, D_MODEL), lambda i, j: (j, 0)), vec, vec],
        out_specs=[pl.BlockSpec((tm, D_MODEL), row), pl.BlockSpec((tm, D_MODEL), row)],
        out_shape=[jax.ShapeDtypeStruct((n, D_MODEL), F32), jax.ShapeDtypeStruct((n, D_MODEL), BF16)],
        scratch_shapes=[pltpu.VMEM((tm, D_MODEL), BF16), pltpu.VMEM((tm, D_MODEL), F32)],
        compiler_params=_params("parallel", "arbitrary"),
        name="merge_out_ln",
    )(x2, o_a, o_b, o_m, w_gate, w_gate, w_gate, b_gate, b_gate, b_gate, w_pa, w_pb, w_pm, w_out,
      ln_g, ln_b)


def _take_top(s, count, row_id):
    order = jnp.full(s.shape, float(count), F32)
    vals, ids = [], []
    for a in range(count):
        m = jnp.max(s, axis=0, keepdims=True)
        rid = jnp.min(jnp.where(s == m, row_id, _NO_ROW), axis=0, keepdims=True)
        hit = row_id == rid
        order = jnp.where(hit, float(a), order)
        s = jnp.where(hit, -jnp.inf, s)
        vals.append(m)
        ids.append(rid)
    return vals, ids, order


def _sorting_network(n):
    pairs = []
    p = 1
    while p < n:
        k = p
        while k >= 1:
            for j in range(k % p, n - k, 2 * k):
                for i in range(min(k, n - j - k)):
                    if (i + j) // (2 * p) == (i + j + k) // (2 * p):
                        pairs.append((i + j, i + j + k))
            k //= 2
        p *= 2
    return pairs


def _top_sorted(s):
    assert s.shape[0] == PEER_TOPK * SUBLANES
    v = [s[g * SUBLANES:(g + 1) * SUBLANES] for g in range(PEER_TOPK)]

    def exchange(i, j):
        v[i], v[j] = jnp.maximum(v[i], v[j]), jnp.minimum(v[i], v[j])

    for i, j in _sorting_network(PEER_TOPK):
        exchange(i, j)
    shift = SUBLANES // 2
    while shift >= 1:
        other = [pltpu.roll(x, shift, 0) for x in v]
        v = [jnp.maximum(v[i], other[PEER_TOPK - 1 - i]) for i in range(PEER_TOPK)]
        d = PEER_TOPK // 2
        while d >= 1:
            for i in range(PEER_TOPK):
                if not i & d:
                    exchange(i, i + d)
            d //= 2
        shift //= 2
    return v


def _count_greater(t, x):
    assert len(t) == 16
    b3 = t[7] > x
    b2 = jnp.where(b3, t[11], t[3]) > x
    b1 = jnp.where(b3, jnp.where(b2, t[13], t[9]), jnp.where(b2, t[5], t[1])) > x
    upper = jnp.where(b2, jnp.where(b1, t[14], t[12]), jnp.where(b1, t[10], t[8]))
    lower = jnp.where(b2, jnp.where(b1, t[6], t[4]), jnp.where(b1, t[2], t[0]))
    b0 = jnp.where(b3, upper, lower) > x
    count = (jnp.where(b3, 8.0, 0.0) + jnp.where(b2, 4.0, 0.0) + jnp.where(b1, 2.0, 0.0)
             + jnp.where(b0, 1.0, 0.0))
    return jnp.where(t[15] > x, 16.0, count)


def _top_untied(s):
    t = _top_sorted(s)
    groups = [s[g * SUBLANES:(g + 1) * SUBLANES] for g in range(s.shape[0] // SUBLANES)]
    order = jnp.concatenate([_count_greater(t, x) for x in groups], axis=0)
    tied = jnp.zeros_like(t[0])
    for a in range(PEER_TOPK - 1):
        tied = jnp.maximum(tied, jnp.where(t[a] == t[a + 1], 1.0, 0.0))
    reach = jnp.zeros_like(t[0])
    for x in groups:
        reach = reach + jnp.where(x >= t[PEER_TOPK - 1], 1.0, 0.0)
    shift = SUBLANES // 2
    while shift >= 1:
        reach = reach + pltpu.roll(reach, shift, 0)
        shift //= 2
    tied = jnp.maximum(tied, jnp.where(reach != float(PEER_TOPK), 1.0, 0.0))
    return [x[0:1] for x in t], order, tied


_NO_ROW = 1e9
_CAND_COUNTS = tuple(PEER_TOPK // (a + 1) for a in range(PEER_TOPK))
_CAND_ROWS = -(-sum(_CAND_COUNTS) // SUBLANES) * SUBLANES


def _candidate_ids():
    ids = [a * PEER_TOPK + b for a, nb in enumerate(_CAND_COUNTS) for b in range(nb)]
    ids += [_NO_ROW] * (_CAND_ROWS - len(ids))
    return jnp.broadcast_to(jnp.asarray(ids, F32)[:, None], (_CAND_ROWS, LANES))


def _peer_prep_kernel(xb_ref, wq_ref, k1_ref, k2_ref, cid_ref, n1_ref, e1_ref, r2_ref, e2_ref,
                      st_ref):
    tm = xb_ref.shape[0]
    q = _dot(xb_ref[...], wq_ref[...]).astype(BF16)
    for h in range(PEER_HEADS):
        q1 = q[:, h * PEER_QDIM:h * PEER_QDIM + PEER_HALF]
        q2 = q[:, h * PEER_QDIM + PEER_HALF:(h + 1) * PEER_QDIM]
        st_ref[2 * h] = _dot_nt(k1_ref[...], q1)
        st_ref[2 * h + 1] = _dot_nt(k2_ref[...], q2)
    lane_chunks = tm // LANES

    def body(it, carry):
        h = it // lane_chunks
        lanes = pl.ds(pl.multiple_of((it % lane_chunks) * LANES, LANES), LANES)
        s1 = st_ref[2 * h, :, lanes]
        s2 = st_ref[2 * h + 1, :, lanes]
        t1, order1, tied1 = _top_untied(s1)
        t2, order2, tied2 = _top_untied(s2)

        def with_ties():
            key_id = lax.broadcasted_iota(jnp.int32, (N_KEYS, LANES), 0).astype(F32)
            v1, _, o1 = _take_top(s1, PEER_TOPK, key_id)
            v2, _, o2 = _take_top(s2, PEER_TOPK, key_id)
            return tuple(v1), o1, tuple(v2), o2

        t1, order1, t2, order2 = lax.cond(jnp.max(jnp.maximum(tied1, tied2)) > 0.0, with_ties,
                                          lambda: (tuple(t1), order1, tuple(t2), order2))
        t2_all = jnp.concatenate(t2, axis=0)
        pad = jnp.full((_CAND_ROWS - sum(_CAND_COUNTS), LANES), -jnp.inf, F32)
        cand = jnp.concatenate(
            [t1[a] + t2_all[:nb] for a, nb in enumerate(_CAND_COUNTS)] + [pad], axis=0)
        top, cid, _ = _take_top(cand, PEER_TOPK, cid_ref[...])
        a_id = lax.broadcasted_iota(jnp.int32, (PEER_TOPK, LANES), 0).astype(F32)
        taken = jnp.zeros((PEER_TOPK, LANES), F32)
        z = jnp.zeros((1, LANES), F32)
        for k in range(PEER_TOPK):
            taken = taken + jnp.where(a_id == jnp.floor(cid[k] * (1.0 / PEER_TOPK)), 1.0, 0.0)
            z = z + jnp.exp(top[k] - top[0])
        n1 = jnp.zeros((N_KEYS, LANES), F32)
        for a in range(PEER_TOPK):
            n1 = jnp.where(order1 == float(a), taken[a:a + 1, :], n1)
        keys = pl.ds(pl.multiple_of(h * N_KEYS, N_KEYS), N_KEYS)
        n1_ref[keys, lanes] = n1
        e1_ref[keys, lanes] = jnp.exp(s1 - t1[0])
        r2_ref[keys, lanes] = order2.astype(r2_ref.dtype)
        e2_ref[keys, lanes] = (jnp.exp(s2 - t2[0]) / z).astype(e2_ref.dtype)
        return carry

    lax.fori_loop(0, PEER_HEADS * lane_chunks, body, 0)


def _peer_prep(x1b, w_q, keys1, keys2, tm=512):
    n = x1b.shape[0]
    by_key1 = jax.ShapeDtypeStruct((PEER_HEADS * N_KEYS, n), F32)
    by_key2 = jax.ShapeDtypeStruct((PEER_HEADS * N_KEYS, n), BF16)
    spec = pl.BlockSpec((PEER_HEADS * N_KEYS, tm), lambda i: (0, i))
    return pl.pallas_call(
        _peer_prep_kernel,
        grid=(n // tm,),
        in_specs=[pl.BlockSpec((tm, D_MODEL), lambda i: (i, 0)),
                  _resident((D_MODEL, PEER_HEADS * PEER_QDIM)),
                  _resident((N_KEYS, PEER_HALF)), _resident((N_KEYS, PEER_HALF)),
                  _resident((_CAND_ROWS, LANES))],
        out_specs=[spec] * 4,
        out_shape=[by_key1, by_key1, by_key2, by_key2],
        scratch_shapes=[pltpu.VMEM((2 * PEER_HEADS, N_KEYS, tm), F32)],
        compiler_params=_params("parallel"),
        name="peer_retrieve",
    )(x1b, w_q, keys1, keys2, _candidate_ids())


def _peer_kernel(xb_ref, u_ref, vt_ref, n1_ref, e1_ref, r2_ref, e2_ref, x_ref, g_ref, b_ref,
                 y_ref, acc_ref, a_ref, w_ref, *, alpha):
    j = pl.program_id(1)
    te, tm = a_ref.shape

    @pl.when(j == 0)
    def _():
        acc_ref[...] = jnp.zeros_like(acc_ref)

    a_ref[...] = _dot_nt(u_ref[...], xb_ref[...])
    keys_per_step = te // N_KEYS
    first_key = pl.multiple_of(j * keys_per_step, SUBLANES)

    def first_key_row(ref, h, sub, ls):
        return ref[pl.ds(h * N_KEYS + first_key, SUBLANES), ls][sub:sub + 1]

    chunk = 2 * LANES
    for sub in range(keys_per_step):
        es = slice(sub * N_KEYS, (sub + 1) * N_KEYS)
        for tc in range(tm // chunk):
            ls = slice(tc * chunk, (tc + 1) * chunk)
            gate = jnp.zeros((N_KEYS, chunk), BF16)
            for h in range(PEER_HEADS):
                ks = slice(h * N_KEYS, (h + 1) * N_KEYS)
                n1 = jnp.broadcast_to(first_key_row(n1_ref, h, sub, ls).astype(BF16), gate.shape)
                e1 = jnp.broadcast_to(first_key_row(e1_ref, h, sub, ls).astype(BF16), gate.shape)
                gate = gate + jnp.where(r2_ref[ks, ls] < n1, e2_ref[ks, ls] * e1,
                                        jnp.zeros_like(gate))
            a = a_ref[es, ls]
            gelu = 0.5 * a * (1.0 + lax.erf(a * (2.0 ** -0.5)))
            w_ref[es, ls] = gelu.astype(BF16) * gate

    acc_ref[...] += _dot(vt_ref[...], w_ref[...])

    @pl.when(j == pl.num_programs(1) - 1)
    def _():
        ff = acc_ref[...].T
        y_ref[...] = _layer_norm(alpha * x_ref[...] + ff, g_ref[...], b_ref[...])


def _peer(x1, x1b, u, vt, n1, e1, r2, e2, ln_g, ln_b, alpha, tm=512, te=1024):
    n = x1.shape[0]
    n_experts = u.shape[0]
    assert (te // N_KEYS) % SUBLANES == 0
    row = lambda i, j: (i, 0)
    sel = pl.BlockSpec((PEER_HEADS * N_KEYS, tm), lambda i, j: (0, i),
                       pipeline_mode=pl.Buffered(1))
    vec = pl.BlockSpec((1, D_MODEL), lambda i, j: (0, 0))
    return pl.pallas_call(
        functools.partial(_peer_kernel, alpha=alpha),
        grid=(n // tm, n_experts // te),
        in_specs=[pl.BlockSpec((tm, D_MODEL), row),
                  pl.BlockSpec((te, D_MODEL), lambda i, j: (j, 0)),
                  pl.BlockSpec((D_MODEL, te), lambda i, j: (0, j)),
                  sel, sel, sel, sel,
                  pl.BlockSpec((tm, D_MODEL), row, pipeline_mode=pl.Buffered(1)), vec, vec],
        out_specs=pl.BlockSpec((tm, D_MODEL), row),
        out_shape=jax.ShapeDtypeStruct((n, D_MODEL), F32),
        scratch_shapes=[pltpu.VMEM((D_MODEL, tm), F32), pltpu.VMEM((te, tm), F32),
                        pltpu.VMEM((te, tm), BF16)],
        compiler_params=_params("parallel", "arbitrary"),
        name="peer_experts_ln",
    )(x1b, u, vt, n1, e1, r2, e2, x1, ln_g, ln_b)


def _encoder_layer(x, mem, p, alpha):
    batch, seq_len, _ = x.shape
    x2 = x.reshape(batch * seq_len, D_MODEL)
    q_a, k_a, v_a, q_b, k_b, v_b, q_m = _project(x2, p["w_qkv"], _rope_tables(seq_len), seq_len)
    o_a = _window_attention(q_a, k_a, v_a, p["a_sink"], batch, seq_len)
    o_b = _neighbourhood_attention(q_b, k_b, v_b, p["na_bias"], batch, seq_len)
    km, vm = _memory_kv(mem.reshape(-1, D_MODEL), p["w_mem_kv"])
    o_m = _memory_attention(q_m, km, vm, batch, seq_len)
    x1, x1b = _merge(x2, o_a, o_b, o_m, p["w_gate"], p["b_gate"], p["w_proj_a"], p["w_proj_b"],
                     p["w_proj_m"], p["w_out"], p["ln1_g"], p["ln1_b"], alpha)
    n1, e1, r2, e2 = _peer_prep(x1b, p["w_peer_q"], p["peer_keys1"], p["peer_keys2"])
    y = _peer(x1, x1b, p["peer_u"], p["peer_vt"], n1, e1, r2, e2, p["ln2_g"], p["ln2_b"], alpha)
    return y.reshape(batch, seq_len, D_MODEL)


def _layer_params(w_in, b_gate, a_sink, na_rpb, w_mem_kv, w_proj_a, w_proj_b, w_proj_m, w_out,
                  ln1_g, ln1_b, w_peer_q, peer_keys1, peer_keys2, peer_u, peer_v, ln2_g, ln2_b):
    row = lambda v: v.reshape(1, -1).astype(F32)
    return {
        "w_qkv": w_in[:, :QKV_W].astype(BF16),
        "w_gate": w_in[:, QKV_W:].astype(BF16),
        "b_gate": row(b_gate),
        "a_sink": a_sink.astype(F32),
        "na_bias": _na_bias_table(na_rpb),
        "w_mem_kv": w_mem_kv.astype(BF16),
        "w_proj_a": w_proj_a.astype(BF16),
        "w_proj_b": w_proj_b.astype(BF16),
        "w_proj_m": w_proj_m.astype(BF16),
        "w_out": w_out.astype(BF16),
        "ln1_g": row(ln1_g),
        "ln1_b": row(ln1_b),
        "w_peer_q": w_peer_q.astype(BF16),
        "peer_keys1": peer_keys1.astype(BF16),
        "peer_keys2": peer_keys2.astype(BF16),
        "peer_u": peer_u.astype(BF16),
        "peer_vt": peer_v.T.astype(BF16),
        "ln2_g": row(ln2_g),
        "ln2_b": row(ln2_b),
    }


def kernel(x_prompt, x_sample, mem_prompt, mem_sample, w_in, b_gate, a_sink, na_rpb, w_mem_kv,
           w_proj_a, w_proj_b, w_proj_m, w_out, ln1_g, ln1_b, w_peer_q, peer_keys1, peer_keys2,
           peer_u, peer_v, ln2_g, ln2_b):
    depth = w_in.shape[0]
    alpha = (2.0 * depth) ** 0.25
    y_prompt, y_sample = x_prompt, x_sample
    for l in range(depth):
        p = _layer_params(w_in[l], b_gate[l], a_sink[l], na_rpb[l], w_mem_kv[l], w_proj_a[l],
                          w_proj_b[l], w_proj_m[l], w_out[l], ln1_g[l], ln1_b[l], w_peer_q[l],
                          peer_keys1[l], peer_keys2[l], peer_u[l], peer_v[l], ln2_g[l], ln2_b[l])
        y_prompt = _encoder_layer(y_prompt, mem_prompt, p, alpha)
        y_sample = _encoder_layer(y_sample, mem_sample, p, alpha)
    return (y_prompt, y_sample)
```

```python
import functools

import jax
import jax.numpy as jnp
from jax import lax
from jax.experimental import pallas as pl
from jax.experimental.pallas import tpu as pltpu

D_MODEL = 2048
HEAD_DIM = 128
A_HEADS = 8
A_KV_HEADS = 2
A_GROUP = A_HEADS // A_KV_HEADS
WINDOW = 128
ROPE_THETA = 500000.0
ROPE_DIM = HEAD_DIM // 4
B_HEADS = 4
GRID_W = 64
NA_ROWS = 8
NA_COLS = 16
M_HEADS = 4
N_BRANCH = 3
A_Q_W = A_HEADS * HEAD_DIM
A_KV_W = A_KV_HEADS * HEAD_DIM
B_W = B_HEADS * HEAD_DIM
M_W = M_HEADS * HEAD_DIM
QKV_W = A_Q_W + 2 * A_KV_W + 3 * B_W + M_W
PEER_HEADS = 8
PEER_QDIM = 256
PEER_HALF = PEER_QDIM // 2
N_KEYS = 128
PEER_TOPK = 16
LN_EPS = 1e-5
NEG_INF = -1e30
ATTN_SCALE = HEAD_DIM ** -0.5

LANES = 128
SUBLANES = 8
VMEM_LIMIT = 56 * 1024 * 1024

BF16 = jnp.bfloat16
F32 = jnp.float32

_PROJ_PIECES = (("q_a", A_Q_W, True), ("k_a", A_KV_W, True), ("v_a", A_KV_W, False),
                ("q_b", B_W, False), ("k_b", B_W, False), ("v_b", B_W, False), ("q_m", M_W, False))
_COL_CHUNK = 512


def _dot(a, b):
    return jnp.dot(a, b, preferred_element_type=F32)


def _dot_nt(a, b):
    return lax.dot_general(a, b, (((1,), (1,)), ((), ())), preferred_element_type=F32)


def _params(*semantics):
    return pltpu.CompilerParams(dimension_semantics=semantics, vmem_limit_bytes=VMEM_LIMIT)


def _resident(shape):
    return pl.BlockSpec(shape, lambda *_: (0,) * len(shape), pipeline_mode=pl.Buffered(1))


def _layer_norm(h, g, b):
    mu = jnp.mean(h, axis=-1, keepdims=True)
    hc = h - mu
    var = jnp.mean(hc * hc, axis=-1, keepdims=True)
    return hc * lax.rsqrt(var + LN_EPS) * g + b


def _proj_kernel(x_ref, w_ref, cos_ref, sin_up_ref, sin_dn_ref, *out_refs):
    xb = x_ref[...].astype(BF16)
    off = 0
    for (_, width, rotary), o_ref in zip(_PROJ_PIECES, out_refs):
        for c0 in range(0, width, _COL_CHUNK):
            cw = min(_COL_CHUNK, width - c0)
            acc = _dot(xb, w_ref[:, off + c0:off + c0 + cw])
            if rotary:
                reps = cw // HEAD_DIM
                cos = jnp.concatenate([cos_ref[...]] * reps, axis=1)
                sin_up = jnp.concatenate([sin_up_ref[...]] * reps, axis=1)
                sin_dn = jnp.concatenate([sin_dn_ref[...]] * reps, axis=1)
                half = ROPE_DIM // 2
                acc = (acc * cos + pltpu.roll(acc, half, 1) * sin_up
                       + pltpu.roll(acc, cw - half, 1) * sin_dn)
            o_ref[:, c0:c0 + cw] = acc.astype(o_ref.dtype)
        off += width


def _rope_tables(seq_len):
    half = ROPE_DIM // 2
    inv_freq = ROPE_THETA ** (-jnp.arange(half, dtype=F32) * 2.0 / ROPE_DIM)
    ang = jnp.arange(seq_len, dtype=F32)[:, None] * inv_freq[None, :]
    cos, sin = jnp.cos(ang), jnp.sin(ang)
    zeros = jnp.zeros((seq_len, HEAD_DIM - ROPE_DIM), F32)
    zh = jnp.zeros((seq_len, half), F32)
    cos_t = jnp.concatenate([cos, cos, zeros + 1.0], axis=1)
    sin_up = jnp.concatenate([zh, sin, zeros], axis=1)
    sin_dn = jnp.concatenate([-sin, zh, zeros], axis=1)
    return cos_t, sin_up, sin_dn


def _project(x2, w_qkv, rope, seq_len, tm=512):
    n = x2.shape[0]
    blocks_per_seq = seq_len // tm
    tab_spec = pl.BlockSpec((tm, HEAD_DIM), lambda i: (i % blocks_per_seq, 0))
    return pl.pallas_call(
        _proj_kernel,
        grid=(n // tm,),
        in_specs=[pl.BlockSpec((tm, D_MODEL), lambda i: (i, 0)),
                  _resident((D_MODEL, QKV_W)), tab_spec, tab_spec, tab_spec],
        out_specs=[pl.BlockSpec((tm, w), lambda i: (i, 0)) for _, w, _ in _PROJ_PIECES],
        out_shape=[jax.ShapeDtypeStruct((n, w), BF16) for _, w, _ in _PROJ_PIECES],
        compiler_params=_params("parallel"),
        name="qkv_proj",
    )(x2, w_qkv, *rope)


_WIN_CHUNK = 512
_WIN_SUB = _WIN_CHUNK // WINDOW


def _win_attn_kernel(sink_ref, q_ref, kp_ref, kc_ref, kn_ref, vp_ref, vc_ref, vn_ref, o_ref):
    c = pl.program_id(1)
    rows = A_GROUP * WINDOW
    qi = lax.broadcasted_iota(jnp.int32, (rows, WINDOW), 0) % WINDOW
    kj = lax.broadcasted_iota(jnp.int32, (rows, WINDOW), 1)
    below = kj >= qi
    above = kj <= qi
    prev_pad = jnp.where(c > 0, 0.0, NEG_INF)
    next_pad = jnp.where(c < pl.num_programs(1) - 1, 0.0, NEG_INF)
    for g in range(A_KV_HEADS):
        hs = slice(g * HEAD_DIM, (g + 1) * HEAD_DIM)
        sink = jnp.concatenate(
            [jnp.full((WINDOW, 1), sink_ref[A_GROUP * g + u], F32) for u in range(A_GROUP)], axis=0)
        for sb in range(_WIN_SUB):
            rs = slice(sb * WINDOW, (sb + 1) * WINDOW)
            q = jnp.concatenate(
                [q_ref[rs, (A_GROUP * g + u) * HEAD_DIM:(A_GROUP * g + u + 1) * HEAD_DIM]
                 for u in range(A_GROUP)], axis=0)
            if sb == 0:
                k_prev, v_prev, pad_prev = kp_ref[:, hs], vp_ref[:, hs], prev_pad
            else:
                ps = slice((sb - 1) * WINDOW, sb * WINDOW)
                k_prev, v_prev, pad_prev = kc_ref[ps, hs], vc_ref[ps, hs], 0.0
            if sb == _WIN_SUB - 1:
                k_next, v_next, pad_next = kn_ref[:, hs], vn_ref[:, hs], next_pad
            else:
                ns = slice((sb + 1) * WINDOW, (sb + 2) * WINDOW)
                k_next, v_next, pad_next = kc_ref[ns, hs], vc_ref[ns, hs], 0.0
            s_prev = jnp.where(below, _dot_nt(q, k_prev) * ATTN_SCALE, NEG_INF) + pad_prev
            s_cur = _dot_nt(q, kc_ref[rs, hs]) * ATTN_SCALE
            s_next = jnp.where(above, _dot_nt(q, k_next) * ATTN_SCALE, NEG_INF) + pad_next
            m = jnp.max(jnp.maximum(jnp.maximum(s_prev, s_cur), s_next), axis=-1, keepdims=True)
            m = jnp.maximum(m, sink)
            p_prev, p_cur, p_next = jnp.exp(s_prev - m), jnp.exp(s_cur - m), jnp.exp(s_next - m)
            denom = jnp.sum(p_prev + p_cur + p_next, axis=-1, keepdims=True) + jnp.exp(sink - m)
            o = (_dot(p_prev.astype(BF16), v_prev) + _dot(p_cur.astype(BF16), vc_ref[rs, hs])
                 + _dot(p_next.astype(BF16), v_next)) / denom
            for u in range(A_GROUP):
                h = A_GROUP * g + u
                o_ref[rs, h * HEAD_DIM:(h + 1) * HEAD_DIM] = (
                    o[u * WINDOW:(u + 1) * WINDOW].astype(o_ref.dtype))


def _window_attention(q, k, v, sink, batch, seq_len):
    n = q.shape[0]
    chunks = seq_len // _WIN_CHUNK
    blocks = seq_len // WINDOW

    def cur(b, c):
        return (b * chunks + c, 0)

    def prev(b, c):
        return (b * blocks + jnp.maximum(c * _WIN_SUB - 1, 0), 0)

    def nxt(b, c):
        return (b * blocks + jnp.minimum((c + 1) * _WIN_SUB, blocks - 1), 0)

    def kv_specs():
        return [pl.BlockSpec((WINDOW, A_KV_W), prev), pl.BlockSpec((_WIN_CHUNK, A_KV_W), cur),
                pl.BlockSpec((WINDOW, A_KV_W), nxt)]

    return pl.pallas_call(
        _win_attn_kernel,
        grid=(batch, chunks),
        in_specs=[pl.BlockSpec(memory_space=pltpu.SMEM), pl.BlockSpec((_WIN_CHUNK, A_Q_W), cur),
                  *kv_specs(), *kv_specs()],
        out_specs=pl.BlockSpec((_WIN_CHUNK, A_Q_W), cur),
        out_shape=jax.ShapeDtypeStruct((n, A_Q_W), BF16),
        compiler_params=_params("parallel", "parallel"),
        name="window_attn",
    )(sink, q, k, k, k, v, v, v)


_NA_KEYS = NA_ROWS * GRID_W
_NA_CHUNK = NA_ROWS * GRID_W
_NA_HALO = _NA_CHUNK // 2


def _na_bias_table(rpb):
    c = jnp.arange(GRID_W)
    cs = jnp.clip(c - NA_COLS // 2, 0, GRID_W - NA_COLS)
    col_ok = (c[None, :] >= cs[:, None]) & (c[None, :] < cs[:, None] + NA_COLS)
    dc = jnp.clip(c[None, :] - c[:, None], -(NA_COLS - 1), NA_COLS - 1) + NA_COLS - 1
    onehot = dc[:, :, None] == jnp.arange(2 * NA_COLS - 1)[None, None, :]
    by_col = jnp.sum(jnp.where(onehot[None, None], rpb.astype(F32)[:, :, None, None, :], 0.0), axis=-1)
    by_col = jnp.where(col_ok[None, None], by_col, NEG_INF)
    pats = [by_col[:, NA_ROWS - 1 - p:2 * NA_ROWS - 1 - p] for p in range(NA_ROWS)]
    bias = jnp.stack(pats, axis=0)
    return bias.transpose(0, 1, 3, 2, 4).reshape(NA_ROWS, B_HEADS, GRID_W, _NA_KEYS)


def _na_kernel(bias_ref, q_ref, kp_ref, kc_ref, kn_ref, vp_ref, vc_ref, vn_ref, o_ref,
               kbuf, vbuf, *, grid_rows):
    c = pl.program_id(1)
    kbuf[0:_NA_HALO] = kp_ref[...]
    kbuf[_NA_HALO:_NA_HALO + _NA_CHUNK] = kc_ref[...]
    kbuf[_NA_HALO + _NA_CHUNK:] = kn_ref[...]
    vbuf[0:_NA_HALO] = vp_ref[...]
    vbuf[_NA_HALO:_NA_HALO + _NA_CHUNK] = vc_ref[...]
    vbuf[_NA_HALO + _NA_CHUNK:] = vn_ref[...]
    halo_rows = _NA_HALO // GRID_W
    for i in range(NA_ROWS):
        r = c * NA_ROWS + i
        rs = jnp.clip(r - NA_ROWS // 2, 0, grid_rows - NA_ROWS)
        pat = r - rs
        start = pl.multiple_of((rs - (c * NA_ROWS - halo_rows)) * GRID_W, GRID_W)
        qs = slice(i * GRID_W, (i + 1) * GRID_W)
        for h in range(B_HEADS):
            hs = slice(h * HEAD_DIM, (h + 1) * HEAD_DIM)
            kw = kbuf[pl.ds(start, _NA_KEYS), hs]
            vw = vbuf[pl.ds(start, _NA_KEYS), hs]
            s = _dot_nt(q_ref[qs, hs], kw) * ATTN_SCALE + bias_ref[pat, h]
            m = jnp.max(s, axis=-1, keepdims=True)
            p = jnp.exp(s - m)
            denom = jnp.sum(p, axis=-1, keepdims=True)
            o_ref[qs, hs] = (_dot(p.astype(BF16), vw) / denom).astype(o_ref.dtype)


def _neighbourhood_attention(q, k, v, bias, batch, seq_len):
    n = q.shape[0]
    grid_rows = seq_len // GRID_W
    assert grid_rows >= 2 * NA_ROWS and seq_len % _NA_CHUNK == 0
    chunks = seq_len // _NA_CHUNK
    halos = seq_len // _NA_HALO

    def cur(b, c):
        return (b * chunks + c, 0)

    def prev(b, c):
        return (b * halos + jnp.maximum(2 * c - 1, 0), 0)

    def nxt(b, c):
        return (b * halos + jnp.minimum(2 * c + 2, halos - 1), 0)

    def kv_specs():
        return [pl.BlockSpec((_NA_HALO, B_W), prev), pl.BlockSpec((_NA_CHUNK, B_W), cur),
                pl.BlockSpec((_NA_HALO, B_W), nxt)]

    buf = pltpu.VMEM((_NA_CHUNK + 2 * _NA_HALO, B_W), BF16)
    return pl.pallas_call(
        functools.partial(_na_kernel, grid_rows=grid_rows),
        grid=(batch, chunks),
        in_specs=[_resident((NA_ROWS, B_HEADS, GRID_W, _NA_KEYS)),
                  pl.BlockSpec((_NA_CHUNK, B_W), cur), *kv_specs(), *kv_specs()],
        out_specs=pl.BlockSpec((_NA_CHUNK, B_W), cur),
        out_shape=jax.ShapeDtypeStruct((n, B_W), BF16),
        scratch_shapes=[buf, buf],
        compiler_params=_params("parallel", "parallel"),
        name="neighbourhood_attn",
    )(bias, q, k, k, k, v, v, v)


def _mem_kv_kernel(mem_ref, w_ref, k_ref, v_ref):
    kv = _dot(mem_ref[...].astype(BF16), w_ref[...])
    k_ref[...] = kv[:, :M_W].astype(k_ref.dtype)
    v_ref[...] = kv[:, M_W:].astype(v_ref.dtype)


def _memory_kv(mem2, w_mem_kv, tm=256):
    n = mem2.shape[0]
    out = jax.ShapeDtypeStruct((n, M_W), BF16)
    return pl.pallas_call(
        _mem_kv_kernel,
        grid=(n // tm,),
        in_specs=[pl.BlockSpec((tm, D_MODEL), lambda i: (i, 0)), _resident((D_MODEL, 2 * M_W))],
        out_specs=[pl.BlockSpec((tm, M_W), lambda i: (i, 0))] * 2,
        out_shape=[out, out],
        compiler_params=_params("parallel"),
        name="memory_kv",
    )(mem2, w_mem_kv)


def _mem_attn_kernel(q_ref, k_ref, v_ref, o_ref):
    for h in range(M_HEADS):
        hs = slice(h * HEAD_DIM, (h + 1) * HEAD_DIM)
        s = _dot_nt(q_ref[:, hs], k_ref[:, hs]) * ATTN_SCALE
        m = jnp.max(s, axis=-1, keepdims=True)
        p = jnp.exp(s - m)
        denom = jnp.sum(p, axis=-1, keepdims=True)
        o_ref[:, hs] = (_dot(p.astype(BF16), v_ref[:, hs]) / denom).astype(o_ref.dtype)


def _memory_attention(q, km, vm, batch, seq_len, tm=512):
    n = q.shape[0]
    n_mem = km.shape[0] // batch
    chunks = seq_len // tm
    return pl.pallas_call(
        _mem_attn_kernel,
        grid=(batch, chunks),
        in_specs=[pl.BlockSpec((tm, M_W), lambda b, c: (b * chunks + c, 0)),
                  pl.BlockSpec((n_mem, M_W), lambda b, c: (b, 0)),
                  pl.BlockSpec((n_mem, M_W), lambda b, c: (b, 0))],
        out_specs=pl.BlockSpec((tm, M_W), lambda b, c: (b * chunks + c, 0)),
        out_shape=jax.ShapeDtypeStruct((n, M_W), BF16),
        compiler_params=_params("parallel", "parallel"),
        name="memory_attn",
    )(q, km, vm)


_MERGE_COLS = 512


def _merge_kernel(x_ref, oa_ref, ob_ref, om_ref, wg0_ref, wg1_ref, wg2_ref, bg0_ref, bg1_ref,
                  bg2_ref, wa_ref, wb_ref, wm_ref, wo_ref, g_ref, b_ref, y_ref, yb_ref,
                  xb_ref, acc_ref, *, alpha):
    j = pl.program_id(1)

    @pl.when(j == 0)
    def _():
        xb_ref[...] = x_ref[...].astype(BF16)
        acc_ref[...] = jnp.zeros_like(acc_ref)

    xb = xb_ref[...]
    merged = (jax.nn.sigmoid(_dot(xb, wg0_ref[...]) + bg0_ref[...]) * _dot(oa_ref[...], wa_ref[...])
              + jax.nn.sigmoid(_dot(xb, wg1_ref[...]) + bg1_ref[...]) * _dot(ob_ref[...], wb_ref[...])
              + jax.nn.sigmoid(_dot(xb, wg2_ref[...]) + bg2_ref[...]) * _dot(om_ref[...], wm_ref[...]))
    acc_ref[...] += _dot(merged.astype(BF16), wo_ref[...])

    @pl.when(j == pl.num_programs(1) - 1)
    def _():
        y = _layer_norm(alpha * x_ref[...] + acc_ref[...], g_ref[...], b_ref[...])
        y_ref[...] = y
        yb_ref[...] = y.astype(BF16)


def _merge(x2, o_a, o_b, o_m, w_gate, b_gate, w_pa, w_pb, w_pm, w_out, ln_g, ln_b, alpha, tm=512):
    n = x2.shape[0]
    steps = D_MODEL // _MERGE_COLS
    row = lambda i, j: (i, 0)

    def gate_w(branch):
        return pl.BlockSpec((D_MODEL, _MERGE_COLS), lambda i, j: (0, branch * steps + j))

    def gate_b(branch):
        return pl.BlockSpec((1, _MERGE_COLS), lambda i, j: (0, branch * steps + j))

    col = lambda i, j: (0, j)
    vec = pl.BlockSpec((1, D_MODEL), lambda i, j: (0, 0))

    def once(shape):
        return pl.BlockSpec(shape, row, pipeline_mode=pl.Buffered(1))

    return pl.pallas_call(
        functools.partial(_merge_kernel, alpha=alpha),
        grid=(n // tm, steps),
        in_specs=[once((tm, D_MODEL)), once((tm, A_Q_W)), once((tm, B_W)), once((tm, M_W)),
                  gate_w(0), gate_w(1), gate_w(2), gate_b(0), gate_b(1), gate_b(2),
                  pl.BlockSpec((A_Q_W, _MERGE_COLS), col), pl.BlockSpec((B_W, _MERGE_COLS), col),
                  pl.BlockSpec((M_W, _MERGE_COLS), col),
                  pl.BlockSpec((_MERGE_COLS, D_MODEL), lambda i, j: (j, 0)), vec, vec],
        out_specs=[pl.BlockSpec((tm, D_MODEL), row), pl.BlockSpec((tm, D_MODEL), row)],
        out_shape=[jax.ShapeDtypeStruct((n, D_MODEL), F32), jax.ShapeDtypeStruct((n, D_MODEL), BF16)],
        scratch_shapes=[pltpu.VMEM((tm, D_MODEL), BF16), pltpu.VMEM((tm, D_MODEL), F32)],
        compiler_params=_params("parallel", "arbitrary"),
        name="merge_out_ln",
    )(x2, o_a, o_b, o_m, w_gate, w_gate, w_gate, b_gate, b_gate, b_gate, w_pa, w_pb, w_pm, w_out,
      ln_g, ln_b)


def _take_top(s, count, row_id):
    order = jnp.full(s.shape, float(count), F32)
    vals, ids = [], []
    for a in range(count):
        m = jnp.max(s, axis=0, keepdims=True)
        rid = jnp.min(jnp.where(s == m, row_id, _NO_ROW), axis=0, keepdims=True)
        hit = row_id == rid
        order = jnp.where(hit, float(a), order)
        s = jnp.where(hit, -jnp.inf, s)
        vals.append(m)
        ids.append(rid)
    return vals, ids, order


def _sorting_network(n):
    pairs = []
    p = 1
    while p < n:
        k = p
        while k >= 1:
            for j in range(k % p, n - k, 2 * k):
                for i in range(min(k, n - j - k)):
                    if (i + j) // (2 * p) == (i + j + k) // (2 * p):
                        pairs.append((i + j, i + j + k))
            k //= 2
        p *= 2
    return pairs


def _top_sorted(s):
    assert s.shape[0] == PEER_TOPK * SUBLANES
    v = [s[g * SUBLANES:(g + 1) * SUBLANES] for g in range(PEER_TOPK)]

    def exchange(i, j):
        v[i], v[j] = jnp.maximum(v[i], v[j]), jnp.minimum(v[i], v[j])

    for i, j in _sorting_network(PEER_TOPK):
        exchange(i, j)
    shift = SUBLANES // 2
    while shift >= 1:
        other = [pltpu.roll(x, shift, 0) for x in v]
        v = [jnp.maximum(v[i], other[PEER_TOPK - 1 - i]) for i in range(PEER_TOPK)]
        d = PEER_TOPK // 2
        while d >= 1:
            for i in range(PEER_TOPK):
                if not i & d:
                    exchange(i, i + d)
            d //= 2
        shift //= 2
    return v


def _count_greater(t, x):
    assert len(t) == 16
    b3 = t[7] > x
    b2 = jnp.where(b3, t[11], t[3]) > x
    b1 = jnp.where(b3, jnp.where(b2, t[13], t[9]), jnp.where(b2, t[5], t[1])) > x
    upper = jnp.where(b2, jnp.where(b1, t[14], t[12]), jnp.where(b1, t[10], t[8]))
    lower = jnp.where(b2, jnp.where(b1, t[6], t[4]), jnp.where(b1, t[2], t[0]))
    b0 = jnp.where(b3, upper, lower) > x
    count = (jnp.where(b3, 8.0, 0.0) + jnp.where(b2, 4.0, 0.0) + jnp.where(b1, 2.0, 0.0)
             + jnp.where(b0, 1.0, 0.0))
    return jnp.where(t[15] > x, 16.0, count)


def _top_untied(s):
    t = _top_sorted(s)
    groups = [s[g * SUBLANES:(g + 1) * SUBLANES] for g in range(s.shape[0] // SUBLANES)]
    order = jnp.concatenate([_count_greater(t, x) for x in groups], axis=0)
    tied = jnp.zeros_like(t[0])
    for a in range(PEER_TOPK - 1):
        tied = jnp.maximum(tied, jnp.where(t[a] == t[a + 1], 1.0, 0.0))
    reach = jnp.zeros_like(t[0])
    for x in groups:
        reach = reach + jnp.where(x >= t[PEER_TOPK - 1], 1.0, 0.0)
    shift = SUBLANES // 2
    while shift >= 1:
        reach = reach + pltpu.roll(reach, shift, 0)
        shift //= 2
    tied = jnp.maximum(tied, jnp.where(reach != float(PEER_TOPK), 1.0, 0.0))
    return [x[0:1] for x in t], order, tied


_NO_ROW = 1e9
_CAND_COUNTS = tuple(PEER_TOPK // (a + 1) for a in range(PEER_TOPK))
_CAND_ROWS = -(-sum(_CAND_COUNTS) // SUBLANES) * SUBLANES


_PREP_LANES = 2 * LANES


def _candidate_ids():
    ids = [a * PEER_TOPK + b for a, nb in enumerate(_CAND_COUNTS) for b in range(nb)]
    ids += [_NO_ROW] * (_CAND_ROWS - len(ids))
    return jnp.broadcast_to(jnp.asarray(ids, F32)[:, None], (_CAND_ROWS, _PREP_LANES))


def _peer_prep_kernel(xb_ref, wq_ref, k1_ref, k2_ref, cid_ref, n1_ref, e1_ref, r2_ref, e2_ref,
                      st_ref):
    tm = xb_ref.shape[0]
    q = _dot(xb_ref[...], wq_ref[...]).astype(BF16)
    for h in range(PEER_HEADS):
        q1 = q[:, h * PEER_QDIM:h * PEER_QDIM + PEER_HALF]
        q2 = q[:, h * PEER_QDIM + PEER_HALF:(h + 1) * PEER_QDIM]
        st_ref[2 * h] = _dot_nt(k1_ref[...], q1)
        st_ref[2 * h + 1] = _dot_nt(k2_ref[...], q2)
    width = _PREP_LANES
    lane_chunks = tm // width

    def body(it, carry):
        h = it // lane_chunks
        lanes = pl.ds(pl.multiple_of((it % lane_chunks) * width, width), width)
        s1 = st_ref[2 * h, :, lanes]
        s2 = st_ref[2 * h + 1, :, lanes]
        t1, order1, tied1 = _top_untied(s1)
        t2, order2, tied2 = _top_untied(s2)

        def with_ties():
            key_id = lax.broadcasted_iota(jnp.int32, (N_KEYS, width), 0).astype(F32)
            v1, _, o1 = _take_top(s1, PEER_TOPK, key_id)
            v2, _, o2 = _take_top(s2, PEER_TOPK, key_id)
            return tuple(v1), o1, tuple(v2), o2

        t1, order1, t2, order2 = lax.cond(jnp.max(jnp.maximum(tied1, tied2)) > 0.0, with_ties,
                                          lambda: (tuple(t1), order1, tuple(t2), order2))
        t2_all = jnp.concatenate(t2, axis=0)
        pad = jnp.full((_CAND_ROWS - sum(_CAND_COUNTS), width), -jnp.inf, F32)
        cand = jnp.concatenate(
            [t1[a] + t2_all[:nb] for a, nb in enumerate(_CAND_COUNTS)] + [pad], axis=0)
        top, cid, _ = _take_top(cand, PEER_TOPK, cid_ref[...])
        a_id = lax.broadcasted_iota(jnp.int32, (PEER_TOPK, width), 0).astype(F32)
        taken = jnp.zeros((PEER_TOPK, width), F32)
        z = jnp.zeros((1, width), F32)
        for k in range(PEER_TOPK):
            taken = taken + jnp.where(a_id == jnp.floor(cid[k] * (1.0 / PEER_TOPK)), 1.0, 0.0)
            z = z + jnp.exp(top[k] - top[0])
        n1 = jnp.zeros((N_KEYS, width), F32)
        for a in range(PEER_TOPK):
            n1 = jnp.where(order1 == float(a), taken[a:a + 1, :], n1)
        keys = pl.ds(pl.multiple_of(h * N_KEYS, N_KEYS), N_KEYS)
        n1_ref[keys, lanes] = n1
        e1_ref[keys, lanes] = jnp.exp(s1 - t1[0])
        r2_ref[keys, lanes] = order2.astype(r2_ref.dtype)
        e2_ref[keys, lanes] = (jnp.exp(s2 - t2[0]) / z).astype(e2_ref.dtype)
        return carry

    lax.fori_loop(0, PEER_HEADS * lane_chunks, body, 0)


def _peer_prep(x1b, w_q, keys1, keys2, tm=512):
    n = x1b.shape[0]
    by_key1 = jax.ShapeDtypeStruct((PEER_HEADS * N_KEYS, n), F32)
    by_key2 = jax.ShapeDtypeStruct((PEER_HEADS * N_KEYS, n), BF16)
    spec = pl.BlockSpec((PEER_HEADS * N_KEYS, tm), lambda i: (0, i))
    return pl.pallas_call(
        _peer_prep_kernel,
        grid=(n // tm,),
        in_specs=[pl.BlockSpec((tm, D_MODEL), lambda i: (i, 0)),
                  _resident((D_MODEL, PEER_HEADS * PEER_QDIM)),
                  _resident((N_KEYS, PEER_HALF)), _resident((N_KEYS, PEER_HALF)),
                  _resident((_CAND_ROWS, _PREP_LANES))],
        out_specs=[spec] * 4,
        out_shape=[by_key1, by_key1, by_key2, by_key2],
        scratch_shapes=[pltpu.VMEM((2 * PEER_HEADS, N_KEYS, tm), F32)],
        compiler_params=_params("parallel"),
        name="peer_retrieve",
    )(x1b, w_q, keys1, keys2, _candidate_ids())


def _peer_kernel(xb_ref, u_ref, vt_ref, n1_ref, e1_ref, r2_ref, e2_ref, x_ref, g_ref, b_ref,
                 y_ref, acc_ref, a_ref, w_ref, *, alpha):
    j = pl.program_id(1)
    te, tm = a_ref.shape

    @pl.when(j == 0)
    def _():
        acc_ref[...] = jnp.zeros_like(acc_ref)

    a_ref[...] = _dot_nt(u_ref[...], xb_ref[...])
    keys_per_step = te // N_KEYS
    first_key = pl.multiple_of(j * keys_per_step, SUBLANES)

    def first_key_row(ref, h, sub, ls):
        return ref[pl.ds(h * N_KEYS + first_key, SUBLANES), ls][sub:sub + 1]

    chunk = 2 * LANES

    def packed_rows(row):
        tile_rows = 2 * SUBLANES
        tile = jnp.broadcast_to(row, (tile_rows, chunk)).astype(BF16)
        return jnp.concatenate([tile] * (N_KEYS // tile_rows), axis=0)

    for sub in range(keys_per_step):
        es = slice(sub * N_KEYS, (sub + 1) * N_KEYS)
        for tc in range(tm // chunk):
            ls = slice(tc * chunk, (tc + 1) * chunk)
            gate = jnp.zeros((N_KEYS, chunk), BF16)
            for h in range(PEER_HEADS):
                ks = slice(h * N_KEYS, (h + 1) * N_KEYS)
                n1 = packed_rows(first_key_row(n1_ref, h, sub, ls))
                e1 = packed_rows(first_key_row(e1_ref, h, sub, ls))
                gate = gate + jnp.where(r2_ref[ks, ls] < n1, e2_ref[ks, ls] * e1,
                                        jnp.zeros_like(gate))
            a = a_ref[es, ls]
            gelu = 0.5 * a * (1.0 + lax.erf(a * (2.0 ** -0.5)))
            w_ref[es, ls] = gelu.astype(BF16) * gate

    acc_ref[...] += _dot(vt_ref[...], w_ref[...])

    @pl.when(j == pl.num_programs(1) - 1)
    def _():
        ff = acc_ref[...].T
        y_ref[...] = _layer_norm(alpha * x_ref[...] + ff, g_ref[...], b_ref[...])


def _peer(x1, x1b, u, vt, n1, e1, r2, e2, ln_g, ln_b, alpha, tm=512, te=1024):
    n = x1.shape[0]
    n_experts = u.shape[0]
    assert (te // N_KEYS) % SUBLANES == 0
    row = lambda i, j: (i, 0)
    sel = pl.BlockSpec((PEER_HEADS * N_KEYS, tm), lambda i, j: (0, i),
                       pipeline_mode=pl.Buffered(1))
    vec = pl.BlockSpec((1, D_MODEL), lambda i, j: (0, 0))
    return pl.pallas_call(
        functools.partial(_peer_kernel, alpha=alpha),
        grid=(n // tm, n_experts // te),
        in_specs=[pl.BlockSpec((tm, D_MODEL), row),
                  pl.BlockSpec((te, D_MODEL), lambda i, j: (j, 0)),
                  pl.BlockSpec((D_MODEL, te), lambda i, j: (0, j)),
                  sel, sel, sel, sel,
                  pl.BlockSpec((tm, D_MODEL), row, pipeline_mode=pl.Buffered(1)), vec, vec],
        out_specs=pl.BlockSpec((tm, D_MODEL), row),
        out_shape=jax.ShapeDtypeStruct((n, D_MODEL), F32),
        scratch_shapes=[pltpu.VMEM((D_MODEL, tm), F32), pltpu.VMEM((te, tm), F32),
                        pltpu.VMEM((te, tm), BF16)],
        compiler_params=_params("parallel", "arbitrary"),
        name="peer_experts_ln",
    )(x1b, u, vt, n1, e1, r2, e2, x1, ln_g, ln_b)


def _encoder_layer(x, mem, p, alpha):
    batch, seq_len, _ = x.shape
    x2 = x.reshape(batch * seq_len, D_MODEL)
    q_a, k_a, v_a, q_b, k_b, v_b, q_m = _project(x2, p["w_qkv"], _rope_tables(seq_len), seq_len)
    o_a = _window_attention(q_a, k_a, v_a, p["a_sink"], batch, seq_len)
    o_b = _neighbourhood_attention(q_b, k_b, v_b, p["na_bias"], batch, seq_len)
    km, vm = _memory_kv(mem.reshape(-1, D_MODEL), p["w_mem_kv"])
    o_m = _memory_attention(q_m, km, vm, batch, seq_len)
    x1, x1b = _merge(x2, o_a, o_b, o_m, p["w_gate"], p["b_gate"], p["w_proj_a"], p["w_proj_b"],
                     p["w_proj_m"], p["w_out"], p["ln1_g"], p["ln1_b"], alpha)
    n1, e1, r2, e2 = _peer_prep(x1b, p["w_peer_q"], p["peer_keys1"], p["peer_keys2"])
    y = _peer(x1, x1b, p["peer_u"], p["peer_vt"], n1, e1, r2, e2, p["ln2_g"], p["ln2_b"], alpha)
    return y.reshape(batch, seq_len, D_MODEL)


def _layer_params(w_in, b_gate, a_sink, na_rpb, w_mem_kv, w_proj_a, w_proj_b, w_proj_m, w_out,
                  ln1_g, ln1_b, w_peer_q, peer_keys1, peer_keys2, peer_u, peer_v, ln2_g, ln2_b):
    row = lambda v: v.reshape(1, -1).astype(F32)
    return {
        "w_qkv": w_in[:, :QKV_W].astype(BF16),
        "w_gate": w_in[:, QKV_W:].astype(BF16),
        "b_gate": row(b_gate),
        "a_sink": a_sink.astype(F32),
        "na_bias": _na_bias_table(na_rpb),
        "w_mem_kv": w_mem_kv.astype(BF16),
        "w_proj_a": w_proj_a.astype(BF16),
        "w_proj_b": w_proj_b.astype(BF16),
        "w_proj_m": w_proj_m.astype(BF16),
        "w_out": w_out.astype(BF16),
        "ln1_g": row(ln1_g),
        "ln1_b": row(ln1_b),
        "w_peer_q": w_peer_q.astype(BF16),
        "peer_keys1": peer_keys1.astype(BF16),
        "peer_keys2": peer_keys2.astype(BF16),
        "peer_u": peer_u.astype(BF16),
        "peer_vt": peer_v.T.astype(BF16),
        "ln2_g": row(ln2_g),
        "ln2_b": row(ln2_b),
    }


def kernel(x_prompt, x_sample, mem_prompt, mem_sample, w_in, b_gate, a_sink, na_rpb, w_mem_kv,
           w_proj_a, w_proj_b, w_proj_m, w_out, ln1_g, ln1_b, w_peer_q, peer_keys1, peer_keys2,
           peer_u, peer_v, ln2_g, ln2_b):
    depth = w_in.shape[0]
    alpha = (2.0 * depth) ** 0.25
    y_prompt, y_sample = x_prompt, x_sample
    for l in range(depth):
        p = _layer_params(w_in[l], b_gate[l], a_sink[l], na_rpb[l], w_mem_kv[l], w_proj_a[l],
                          w_proj_b[l], w_proj_m[l], w_out[l], ln1_g[l], ln1_b[l], w_peer_q[l],
                          peer_keys1[l], peer_keys2[l], peer_u[l], peer_v[l], ln2_g[l], ln2_b[l])
        y_prompt = _encoder_layer(y_prompt, mem_prompt, p, alpha)
        y_sample = _encoder_layer(y_sample, mem_sample, p, alpha)
    return (y_prompt, y_sample)
```

```python
import functools

import jax
import jax.numpy as jnp
from jax import lax
from jax.experimental import pallas as pl
from jax.experimental.pallas import tpu as pltpu

D_MODEL = 2048
HEAD_DIM = 128
A_HEADS = 8
A_KV_HEADS = 2
A_GROUP = A_HEADS // A_KV_HEADS
WINDOW = 128
ROPE_THETA = 500000.0
ROPE_DIM = HEAD_DIM // 4
B_HEADS = 4
GRID_W = 64
NA_ROWS = 8
NA_COLS = 16
M_HEADS = 4
N_BRANCH = 3
A_Q_W = A_HEADS * HEAD_DIM
A_KV_W = A_KV_HEADS * HEAD_DIM
B_W = B_HEADS * HEAD_DIM
M_W = M_HEADS * HEAD_DIM
QKV_W = A_Q_W + 2 * A_KV_W + 3 * B_W + M_W
PEER_HEADS = 8
PEER_QDIM = 256
PEER_HALF = PEER_QDIM // 2
N_KEYS = 128
PEER_TOPK = 16
LN_EPS = 1e-5
NEG_INF = -1e30
ATTN_SCALE = HEAD_DIM ** -0.5

LANES = 128
SUBLANES = 8
VMEM_LIMIT = 56 * 1024 * 1024

BF16 = jnp.bfloat16
F32 = jnp.float32

_PROJ_PIECES = (("q_a", A_Q_W, True), ("k_a", A_KV_W, True), ("v_a", A_KV_W, False),
                ("q_b", B_W, False), ("k_b", B_W, False), ("v_b", B_W, False), ("q_m", M_W, False))
_COL_CHUNK = 512


def _dot(a, b):
    return jnp.dot(a, b, preferred_element_type=F32)


def _dot_nt(a, b):
    return lax.dot_general(a, b, (((1,), (1,)), ((), ())), preferred_element_type=F32)


def _params(*semantics):
    return pltpu.CompilerParams(dimension_semantics=semantics, vmem_limit_bytes=VMEM_LIMIT)


def _resident(shape):
    return pl.BlockSpec(shape, lambda *_: (0,) * len(shape), pipeline_mode=pl.Buffered(1))


def _layer_norm(h, g, b):
    mu = jnp.mean(h, axis=-1, keepdims=True)
    hc = h - mu
    var = jnp.mean(hc * hc, axis=-1, keepdims=True)
    return hc * lax.rsqrt(var + LN_EPS) * g + b


def _proj_kernel(x_ref, w_ref, cos_ref, sin_up_ref, sin_dn_ref, *out_refs):
    xb = x_ref[...].astype(BF16)
    off = 0
    for (_, width, rotary), o_ref in zip(_PROJ_PIECES, out_refs):
        for c0 in range(0, width, _COL_CHUNK):
            cw = min(_COL_CHUNK, width - c0)
            acc = _dot(xb, w_ref[:, off + c0:off + c0 + cw])
            if rotary:
                reps = cw // HEAD_DIM
                cos = jnp.concatenate([cos_ref[...]] * reps, axis=1)
                sin_up = jnp.concatenate([sin_up_ref[...]] * reps, axis=1)
                sin_dn = jnp.concatenate([sin_dn_ref[...]] * reps, axis=1)
                half = ROPE_DIM // 2
                acc = (acc * cos + pltpu.roll(acc, half, 1) * sin_up
                       + pltpu.roll(acc, cw - half, 1) * sin_dn)
            o_ref[:, c0:c0 + cw] = acc.astype(o_ref.dtype)
        off += width


def _rope_tables(seq_len):
    half = ROPE_DIM // 2
    inv_freq = ROPE_THETA ** (-jnp.arange(half, dtype=F32) * 2.0 / ROPE_DIM)
    ang = jnp.arange(seq_len, dtype=F32)[:, None] * inv_freq[None, :]
    cos, sin = jnp.cos(ang), jnp.sin(ang)
    zeros = jnp.zeros((seq_len, HEAD_DIM - ROPE_DIM), F32)
    zh = jnp.zeros((seq_len, half), F32)
    cos_t = jnp.concatenate([cos, cos, zeros + 1.0], axis=1)
    sin_up = jnp.concatenate([zh, sin, zeros], axis=1)
    sin_dn = jnp.concatenate([-sin, zh, zeros], axis=1)
    return cos_t, sin_up, sin_dn


def _project(x2, w_qkv, rope, seq_len, tm=512):
    n = x2.shape[0]
    blocks_per_seq = seq_len // tm
    tab_spec = pl.BlockSpec((tm, HEAD_DIM), lambda i: (i % blocks_per_seq, 0))
    return pl.pallas_call(
        _proj_kernel,
        grid=(n // tm,),
        in_specs=[pl.BlockSpec((tm, D_MODEL), lambda i: (i, 0)),
                  _resident((D_MODEL, QKV_W)), tab_spec, tab_spec, tab_spec],
        out_specs=[pl.BlockSpec((tm, w), lambda i: (i, 0)) for _, w, _ in _PROJ_PIECES],
        out_shape=[jax.ShapeDtypeStruct((n, w), BF16) for _, w, _ in _PROJ_PIECES],
        compiler_params=_params("parallel"),
        name="qkv_proj",
    )(x2, w_qkv, *rope)


_WIN_CHUNK = 512
_WIN_SUB = _WIN_CHUNK // WINDOW


def _win_attn_kernel(sink_ref, q_ref, kp_ref, kc_ref, kn_ref, vp_ref, vc_ref, vn_ref, o_ref):
    c = pl.program_id(1)
    rows = A_GROUP * WINDOW
    qi = lax.broadcasted_iota(jnp.int32, (rows, WINDOW), 0) % WINDOW
    kj = lax.broadcasted_iota(jnp.int32, (rows, WINDOW), 1)
    below = kj >= qi
    above = kj <= qi
    prev_pad = jnp.where(c > 0, 0.0, NEG_INF)
    next_pad = jnp.where(c < pl.num_programs(1) - 1, 0.0, NEG_INF)
    for g in range(A_KV_HEADS):
        hs = slice(g * HEAD_DIM, (g + 1) * HEAD_DIM)
        sink = jnp.concatenate(
            [jnp.full((WINDOW, 1), sink_ref[A_GROUP * g + u], F32) for u in range(A_GROUP)], axis=0)
        for sb in range(_WIN_SUB):
            rs = slice(sb * WINDOW, (sb + 1) * WINDOW)
            q = jnp.concatenate(
                [q_ref[rs, (A_GROUP * g + u) * HEAD_DIM:(A_GROUP * g + u + 1) * HEAD_DIM]
                 for u in range(A_GROUP)], axis=0)
            if sb == 0:
                k_prev, v_prev, pad_prev = kp_ref[:, hs], vp_ref[:, hs], prev_pad
            else:
                ps = slice((sb - 1) * WINDOW, sb * WINDOW)
                k_prev, v_prev, pad_prev = kc_ref[ps, hs], vc_ref[ps, hs], 0.0
            if sb == _WIN_SUB - 1:
                k_next, v_next, pad_next = kn_ref[:, hs], vn_ref[:, hs], next_pad
            else:
                ns = slice((sb + 1) * WINDOW, (sb + 2) * WINDOW)
                k_next, v_next, pad_next = kc_ref[ns, hs], vc_ref[ns, hs], 0.0
            s_prev = jnp.where(below, _dot_nt(q, k_prev) * ATTN_SCALE, NEG_INF) + pad_prev
            s_cur = _dot_nt(q, kc_ref[rs, hs]) * ATTN_SCALE
            s_next = jnp.where(above, _dot_nt(q, k_next) * ATTN_SCALE, NEG_INF) + pad_next
            m = jnp.max(jnp.maximum(jnp.maximum(s_prev, s_cur), s_next), axis=-1, keepdims=True)
            m = jnp.maximum(m, sink)
            p_prev, p_cur, p_next = jnp.exp(s_prev - m), jnp.exp(s_cur - m), jnp.exp(s_next - m)
            denom = jnp.sum(p_prev + p_cur + p_next, axis=-1, keepdims=True) + jnp.exp(sink - m)
            o = (_dot(p_prev.astype(BF16), v_prev) + _dot(p_cur.astype(BF16), vc_ref[rs, hs])
                 + _dot(p_next.astype(BF16), v_next)) / denom
            for u in range(A_GROUP):
                h = A_GROUP * g + u
                o_ref[rs, h * HEAD_DIM:(h + 1) * HEAD_DIM] = (
                    o[u * WINDOW:(u + 1) * WINDOW].astype(o_ref.dtype))


def _window_attention(q, k, v, sink, batch, seq_len):
    n = q.shape[0]
    chunks = seq_len // _WIN_CHUNK
    blocks = seq_len // WINDOW

    def cur(b, c):
        return (b * chunks + c, 0)

    def prev(b, c):
        return (b * blocks + jnp.maximum(c * _WIN_SUB - 1, 0), 0)

    def nxt(b, c):
        return (b * blocks + jnp.minimum((c + 1) * _WIN_SUB, blocks - 1), 0)

    def kv_specs():
        return [pl.BlockSpec((WINDOW, A_KV_W), prev), pl.BlockSpec((_WIN_CHUNK, A_KV_W), cur),
                pl.BlockSpec((WINDOW, A_KV_W), nxt)]

    return pl.pallas_call(
        _win_attn_kernel,
        grid=(batch, chunks),
        in_specs=[pl.BlockSpec(memory_space=pltpu.SMEM), pl.BlockSpec((_WIN_CHUNK, A_Q_W), cur),
                  *kv_specs(), *kv_specs()],
        out_specs=pl.BlockSpec((_WIN_CHUNK, A_Q_W), cur),
        out_shape=jax.ShapeDtypeStruct((n, A_Q_W), BF16),
        compiler_params=_params("parallel", "parallel"),
        name="window_attn",
    )(sink, q, k, k, k, v, v, v)


_NA_KEYS = NA_ROWS * GRID_W
_NA_CHUNK = NA_ROWS * GRID_W
_NA_HALO = _NA_CHUNK // 2


def _na_bias_table(rpb):
    c = jnp.arange(GRID_W)
    cs = jnp.clip(c - NA_COLS // 2, 0, GRID_W - NA_COLS)
    col_ok = (c[None, :] >= cs[:, None]) & (c[None, :] < cs[:, None] + NA_COLS)
    dc = jnp.clip(c[None, :] - c[:, None], -(NA_COLS - 1), NA_COLS - 1) + NA_COLS - 1
    onehot = dc[:, :, None] == jnp.arange(2 * NA_COLS - 1)[None, None, :]
    by_col = jnp.sum(jnp.where(onehot[None, None], rpb.astype(F32)[:, :, None, None, :], 0.0), axis=-1)
    by_col = jnp.where(col_ok[None, None], by_col, NEG_INF)
    pats = [by_col[:, NA_ROWS - 1 - p:2 * NA_ROWS - 1 - p] for p in range(NA_ROWS)]
    bias = jnp.stack(pats, axis=0)
    return bias.transpose(0, 1, 3, 2, 4).reshape(NA_ROWS, B_HEADS, GRID_W, _NA_KEYS)


def _na_kernel(bias_ref, q_ref, kp_ref, kc_ref, kn_ref, vp_ref, vc_ref, vn_ref, o_ref,
               kbuf, vbuf, *, grid_rows):
    c = pl.program_id(1)
    kbuf[0:_NA_HALO] = kp_ref[...]
    kbuf[_NA_HALO:_NA_HALO + _NA_CHUNK] = kc_ref[...]
    kbuf[_NA_HALO + _NA_CHUNK:] = kn_ref[...]
    vbuf[0:_NA_HALO] = vp_ref[...]
    vbuf[_NA_HALO:_NA_HALO + _NA_CHUNK] = vc_ref[...]
    vbuf[_NA_HALO + _NA_CHUNK:] = vn_ref[...]
    halo_rows = _NA_HALO // GRID_W
    for i in range(NA_ROWS):
        r = c * NA_ROWS + i
        rs = jnp.clip(r - NA_ROWS // 2, 0, grid_rows - NA_ROWS)
        pat = r - rs
        start = pl.multiple_of((rs - (c * NA_ROWS - halo_rows)) * GRID_W, GRID_W)
        qs = slice(i * GRID_W, (i + 1) * GRID_W)
        for h in range(B_HEADS):
            hs = slice(h * HEAD_DIM, (h + 1) * HEAD_DIM)
            kw = kbuf[pl.ds(start, _NA_KEYS), hs]
            vw = vbuf[pl.ds(start, _NA_KEYS), hs]
            s = _dot_nt(q_ref[qs, hs], kw) * ATTN_SCALE + bias_ref[pat, h]
            m = jnp.max(s, axis=-1, keepdims=True)
            p = jnp.exp(s - m)
            denom = jnp.sum(p, axis=-1, keepdims=True)
            o_ref[qs, hs] = (_dot(p.astype(BF16), vw) / denom).astype(o_ref.dtype)


def _neighbourhood_attention(q, k, v, bias, batch, seq_len):
    n = q.shape[0]
    grid_rows = seq_len // GRID_W
    assert grid_rows >= 2 * NA_ROWS and seq_len % _NA_CHUNK == 0
    chunks = seq_len // _NA_CHUNK
    halos = seq_len // _NA_HALO

    def cur(b, c):
        return (b * chunks + c, 0)

    def prev(b, c):
        return (b * halos + jnp.maximum(2 * c - 1, 0), 0)

    def nxt(b, c):
        return (b * halos + jnp.minimum(2 * c + 2, halos - 1), 0)

    def kv_specs():
        return [pl.BlockSpec((_NA_HALO, B_W), prev), pl.BlockSpec((_NA_CHUNK, B_W), cur),
                pl.BlockSpec((_NA_HALO, B_W), nxt)]

    buf = pltpu.VMEM((_NA_CHUNK + 2 * _NA_HALO, B_W), BF16)
    return pl.pallas_call(
        functools.partial(_na_kernel, grid_rows=grid_rows),
        grid=(batch, chunks),
        in_specs=[_resident((NA_ROWS, B_HEADS, GRID_W, _NA_KEYS)),
                  pl.BlockSpec((_NA_CHUNK, B_W), cur), *kv_specs(), *kv_specs()],
        out_specs=pl.BlockSpec((_NA_CHUNK, B_W), cur),
        out_shape=jax.ShapeDtypeStruct((n, B_W), BF16),
        scratch_shapes=[buf, buf],
        compiler_params=_params("parallel", "parallel"),
        name="neighbourhood_attn",
    )(bias, q, k, k, k, v, v, v)


def _mem_kv_kernel(mem_ref, w_ref, k_ref, v_ref):
    kv = _dot(mem_ref[...].astype(BF16), w_ref[...])
    k_ref[...] = kv[:, :M_W].astype(k_ref.dtype)
    v_ref[...] = kv[:, M_W:].astype(v_ref.dtype)


def _memory_kv(mem2, w_mem_kv, tm=256):
    n = mem2.shape[0]
    out = jax.ShapeDtypeStruct((n, M_W), BF16)
    return pl.pallas_call(
        _mem_kv_kernel,
        grid=(n // tm,),
        in_specs=[pl.BlockSpec((tm, D_MODEL), lambda i: (i, 0)), _resident((D_MODEL, 2 * M_W))],
        out_specs=[pl.BlockSpec((tm, M_W), lambda i: (i, 0))] * 2,
        out_shape=[out, out],
        compiler_params=_params("parallel"),
        name="memory_kv",
    )(mem2, w_mem_kv)


def _mem_attn_kernel(q_ref, k_ref, v_ref, o_ref):
    for h in range(M_HEADS):
        hs = slice(h * HEAD_DIM, (h + 1) * HEAD_DIM)
        s = _dot_nt(q_ref[:, hs], k_ref[:, hs]) * ATTN_SCALE
        m = jnp.max(s, axis=-1, keepdims=True)
        p = jnp.exp(s - m)
        denom = jnp.sum(p, axis=-1, keepdims=True)
        o_ref[:, hs] = (_dot(p.astype(BF16), v_ref[:, hs]) / denom).astype(o_ref.dtype)


def _memory_attention(q, km, vm, batch, seq_len, tm=512):
    n = q.shape[0]
    n_mem = km.shape[0] // batch
    chunks = seq_len // tm
    return pl.pallas_call(
        _mem_attn_kernel,
        grid=(batch, chunks),
        in_specs=[pl.BlockSpec((tm, M_W), lambda b, c: (b * chunks + c, 0)),
                  pl.BlockSpec((n_mem, M_W), lambda b, c: (b, 0)),
                  pl.BlockSpec((n_mem, M_W), lambda b, c: (b, 0))],
        out_specs=pl.BlockSpec((tm, M_W), lambda b, c: (b * chunks + c, 0)),
        out_shape=jax.ShapeDtypeStruct((n, M_W), BF16),
        compiler_params=_params("parallel", "parallel"),
        name="memory_attn",
    )(q, km, vm)


_MERGE_COLS = 512


def _merge_kernel(x_ref, oa_ref, ob_ref, om_ref, wg0_ref, wg1_ref, wg2_ref, bg0_ref, bg1_ref,
                  bg2_ref, wa_ref, wb_ref, wm_ref, wo_ref, g_ref, b_ref, y_ref, yb_ref,
                  xb_ref, acc_ref, *, alpha):
    j = pl.program_id(1)

    @pl.when(j == 0)
    def _():
        xb_ref[...] = x_ref[...].astype(BF16)
        acc_ref[...] = jnp.zeros_like(acc_ref)

    xb = xb_ref[...]
    merged = (jax.nn.sigmoid(_dot(xb, wg0_ref[...]) + bg0_ref[...]) * _dot(oa_ref[...], wa_ref[...])
              + jax.nn.sigmoid(_dot(xb, wg1_ref[...]) + bg1_ref[...]) * _dot(ob_ref[...], wb_ref[...])
              + jax.nn.sigmoid(_dot(xb, wg2_ref[...]) + bg2_ref[...]) * _dot(om_ref[...], wm_ref[...]))
    acc_ref[...] += _dot(merged.astype(BF16), wo_ref[...])

    @pl.when(j == pl.num_programs(1) - 1)
    def _():
        y = _layer_norm(alpha * x_ref[...] + acc_ref[...], g_ref[...], b_ref[...])
        y_ref[...] = y
        yb_ref[...] = y.astype(BF16)


def _merge(x2, o_a, o_b, o_m, w_gate, b_gate, w_pa, w_pb, w_pm, w_out, ln_g, ln_b, alpha, tm=512):
    n = x2.shape[0]
    steps = D_MODEL // _MERGE_COLS
    row = lambda i, j: (i, 0)

    def gate_w(branch):
        return pl.BlockSpec((D_MODEL, _MERGE_COLS), lambda i, j: (0, branch * steps + j))

    def gate_b(branch):
        return pl.BlockSpec((1, _MERGE_COLS), lambda i, j: (0, branch * steps + j))

    col = lambda i, j: (0, j)
    vec = pl.BlockSpec((1, D_MODEL), lambda i, j: (0, 0))

    def once(shape):
        return pl.BlockSpec(shape, row, pipeline_mode=pl.Buffered(1))

    return pl.pallas_call(
        functools.partial(_merge_kernel, alpha=alpha),
        grid=(n // tm, steps),
        in_specs=[pl.BlockSpec((tm, D_MODEL), row), once((tm, A_Q_W)), once((tm, B_W)),
                  once((tm, M_W)),
                  gate_w(0), gate_w(1), gate_w(2), gate_b(0), gate_b(1), gate_b(2),
                  pl.BlockSpec((A_Q_W, _MERGE_COLS), col), pl.BlockSpec((B_W, _MERGE_COLS), col),
                  pl.BlockSpec((M_W, _MERGE_COLS), col),
                  pl.BlockSpec((_MERGE_COLS, D_MODEL), lambda i, j: (j, 0)), vec, vec],
        out_specs=[pl.BlockSpec((tm, D_MODEL), row), pl.BlockSpec((tm, D_MODEL), row)],
        out_shape=[jax.ShapeDtypeStruct((n, D_MODEL), F32), jax.ShapeDtypeStruct((n, D_MODEL), BF16)],
        scratch_shapes=[pltpu.VMEM((tm, D_MODEL), BF16), pltpu.VMEM((tm, D_MODEL), F32)],
        compiler_params=_params("parallel", "arbitrary"),
        name="merge_out_ln",
    )(x2, o_a, o_b, o_m, w_gate, w_gate, w_gate, b_gate, b_gate, b_gate, w_pa, w_pb, w_pm, w_out,
      ln_g, ln_b)


def _take_top(s, count, row_id):
    order = jnp.full(s.shape, float(count), F32)
    vals, ids = [], []
    for a in range(count):
        m = jnp.max(s, axis=0, keepdims=True)
        rid = jnp.min(jnp.where(s == m, row_id, _NO_ROW), axis=0, keepdims=True)
        hit = row_id == rid
        order = jnp.where(hit, float(a), order)
        s = jnp.where(hit, -jnp.inf, s)
        vals.append(m)
        ids.append(rid)
    return vals, ids, order


def _sorting_network(n):
    pairs = []
    p = 1
    while p < n:
        k = p
        while k >= 1:
            for j in range(k % p, n - k, 2 * k):
                for i in range(min(k, n - j - k)):
                    if (i + j) // (2 * p) == (i + j + k) // (2 * p):
                        pairs.append((i + j, i + j + k))
            k //= 2
        p *= 2
    return pairs


def _top_sorted(s):
    assert s.shape[0] == PEER_TOPK * SUBLANES
    v = [s[g * SUBLANES:(g + 1) * SUBLANES] for g in range(PEER_TOPK)]

    def exchange(i, j):
        v[i], v[j] = jnp.maximum(v[i], v[j]), jnp.minimum(v[i], v[j])

    for i, j in _sorting_network(PEER_TOPK):
        exchange(i, j)
    shift = SUBLANES // 2
    while shift >= 1:
        other = [pltpu.roll(x, shift, 0) for x in v]
        v = [jnp.maximum(v[i], other[PEER_TOPK - 1 - i]) for i in range(PEER_TOPK)]
        d = PEER_TOPK // 2
        while d >= 1:
            for i in range(PEER_TOPK):
                if not i & d:
                    exchange(i, i + d)
            d //= 2
        shift //= 2
    return v


def _count_greater(t, x):
    assert len(t) == 16
    b3 = t[7] > x
    b2 = jnp.where(b3, t[11], t[3]) > x
    b1 = jnp.where(b3, jnp.where(b2, t[13], t[9]), jnp.where(b2, t[5], t[1])) > x
    upper = jnp.where(b2, jnp.where(b1, t[14], t[12]), jnp.where(b1, t[10], t[8]))
    lower = jnp.where(b2, jnp.where(b1, t[6], t[4]), jnp.where(b1, t[2], t[0]))
    b0 = jnp.where(b3, upper, lower) > x
    count = (jnp.where(b3, 8.0, 0.0) + jnp.where(b2, 4.0, 0.0) + jnp.where(b1, 2.0, 0.0)
             + jnp.where(b0, 1.0, 0.0))
    return jnp.where(t[15] > x, 16.0, count)


def _top_untied(s):
    t = _top_sorted(s)
    groups = [s[g * SUBLANES:(g + 1) * SUBLANES] for g in range(s.shape[0] // SUBLANES)]
    order = jnp.concatenate([_count_greater(t, x) for x in groups], axis=0)
    tied = jnp.zeros_like(t[0])
    for a in range(PEER_TOPK - 1):
        tied = jnp.maximum(tied, jnp.where(t[a] == t[a + 1], 1.0, 0.0))
    reach = jnp.zeros_like(t[0])
    for x in groups:
        reach = reach + jnp.where(x >= t[PEER_TOPK - 1], 1.0, 0.0)
    shift = SUBLANES // 2
    while shift >= 1:
        reach = reach + pltpu.roll(reach, shift, 0)
        shift //= 2
    tied = jnp.maximum(tied, jnp.where(reach != float(PEER_TOPK), 1.0, 0.0))
    return [x[0:1] for x in t], order, tied


_NO_ROW = 1e9
_CAND_COUNTS = tuple(PEER_TOPK // (a + 1) for a in range(PEER_TOPK))
_CAND_ROWS = -(-sum(_CAND_COUNTS) // SUBLANES) * SUBLANES


_PREP_LANES = 2 * LANES


def _candidate_ids():
    ids = [a * PEER_TOPK + b for a, nb in enumerate(_CAND_COUNTS) for b in range(nb)]
    ids += [_NO_ROW] * (_CAND_ROWS - len(ids))
    return jnp.broadcast_to(jnp.asarray(ids, F32)[:, None], (_CAND_ROWS, _PREP_LANES))


def _peer_prep_kernel(xb_ref, wq_ref, k1_ref, k2_ref, cid_ref, n1_ref, e1_ref, r2_ref, e2_ref,
                      st_ref):
    tm = xb_ref.shape[0]
    q = _dot(xb_ref[...], wq_ref[...]).astype(BF16)
    for h in range(PEER_HEADS):
        q1 = q[:, h * PEER_QDIM:h * PEER_QDIM + PEER_HALF]
        q2 = q[:, h * PEER_QDIM + PEER_HALF:(h + 1) * PEER_QDIM]
        st_ref[2 * h] = _dot_nt(k1_ref[...], q1)
        st_ref[2 * h + 1] = _dot_nt(k2_ref[...], q2)
    width = _PREP_LANES
    lane_chunks = tm // width

    def body(it, carry):
        h = it // lane_chunks
        lanes = pl.ds(pl.multiple_of((it % lane_chunks) * width, width), width)
        s1 = st_ref[2 * h, :, lanes]
        s2 = st_ref[2 * h + 1, :, lanes]
        t1, order1, tied1 = _top_untied(s1)
        t2, order2, tied2 = _top_untied(s2)

        def with_ties():
            key_id = lax.broadcasted_iota(jnp.int32, (N_KEYS, width), 0).astype(F32)
            v1, _, o1 = _take_top(s1, PEER_TOPK, key_id)
            v2, _, o2 = _take_top(s2, PEER_TOPK, key_id)
            return tuple(v1), o1, tuple(v2), o2

        t1, order1, t2, order2 = lax.cond(jnp.max(jnp.maximum(tied1, tied2)) > 0.0, with_ties,
                                          lambda: (tuple(t1), order1, tuple(t2), order2))
        t2_all = jnp.concatenate(t2, axis=0)
        pad = jnp.full((_CAND_ROWS - sum(_CAND_COUNTS), width), -jnp.inf, F32)
        cand = jnp.concatenate(
            [t1[a] + t2_all[:nb] for a, nb in enumerate(_CAND_COUNTS)] + [pad], axis=0)
        top, cid, _ = _take_top(cand, PEER_TOPK, cid_ref[...])
        a_id = lax.broadcasted_iota(jnp.int32, (PEER_TOPK, width), 0).astype(F32)
        taken = jnp.zeros((PEER_TOPK, width), F32)
        z = jnp.zeros((1, width), F32)
        for k in range(PEER_TOPK):
            taken = taken + jnp.where(a_id == jnp.floor(cid[k] * (1.0 / PEER_TOPK)), 1.0, 0.0)
            z = z + jnp.exp(top[k] - top[0])
        n1 = jnp.zeros((N_KEYS, width), F32)
        for a in range(PEER_TOPK):
            n1 = jnp.where(order1 == float(a), taken[a:a + 1, :], n1)
        keys = pl.ds(pl.multiple_of(h * N_KEYS, N_KEYS), N_KEYS)
        n1_ref[keys, lanes] = n1
        e1_ref[keys, lanes] = jnp.exp(s1 - t1[0])
        r2_ref[keys, lanes] = order2.astype(r2_ref.dtype)
        e2_ref[keys, lanes] = (jnp.exp(s2 - t2[0]) / z).astype(e2_ref.dtype)
        return carry

    lax.fori_loop(0, PEER_HEADS * lane_chunks, body, 0)


def _peer_prep(x1b, w_q, keys1, keys2, tm=512):
    n = x1b.shape[0]
    by_key1 = jax.ShapeDtypeStruct((PEER_HEADS * N_KEYS, n), F32)
    by_key2 = jax.ShapeDtypeStruct((PEER_HEADS * N_KEYS, n), BF16)
    spec = pl.BlockSpec((PEER_HEADS * N_KEYS, tm), lambda i: (0, i))
    return pl.pallas_call(
        _peer_prep_kernel,
        grid=(n // tm,),
        in_specs=[pl.BlockSpec((tm, D_MODEL), lambda i: (i, 0)),
                  _resident((D_MODEL, PEER_HEADS * PEER_QDIM)),
                  _resident((N_KEYS, PEER_HALF)), _resident((N_KEYS, PEER_HALF)),
                  _resident((_CAND_ROWS, _PREP_LANES))],
        out_specs=[spec] * 4,
        out_shape=[by_key1, by_key1, by_key2, by_key2],
        scratch_shapes=[pltpu.VMEM((2 * PEER_HEADS, N_KEYS, tm), F32)],
        compiler_params=_params("parallel"),
        name="peer_retrieve",
    )(x1b, w_q, keys1, keys2, _candidate_ids())


def _peer_kernel(xb_ref, u_ref, vt_ref, n1_ref, e1_ref, r2_ref, e2_ref, x_ref, g_ref, b_ref,
                 y_ref, acc_ref, a_ref, w_ref, *, alpha):
    j = pl.program_id(1)
    te, tm = a_ref.shape

    @pl.when(j == 0)
    def _():
        acc_ref[...] = jnp.zeros_like(acc_ref)

    a_ref[...] = _dot_nt(u_ref[...], xb_ref[...])
    keys_per_step = te // N_KEYS
    first_key = pl.multiple_of(j * keys_per_step, SUBLANES)

    def first_key_row(ref, h, sub, ls):
        return ref[pl.ds(h * N_KEYS + first_key, SUBLANES), ls][sub:sub + 1]

    chunk = 2 * LANES

    def packed_rows(row):
        tile_rows = 2 * SUBLANES
        tile = jnp.broadcast_to(row, (tile_rows, chunk)).astype(BF16)
        return jnp.concatenate([tile] * (N_KEYS // tile_rows), axis=0)

    for sub in range(keys_per_step):
        es = slice(sub * N_KEYS, (sub + 1) * N_KEYS)
        for tc in range(tm // chunk):
            ls = slice(tc * chunk, (tc + 1) * chunk)
            gate = jnp.zeros((N_KEYS, chunk), BF16)
            for h in range(PEER_HEADS):
                ks = slice(h * N_KEYS, (h + 1) * N_KEYS)
                n1 = packed_rows(first_key_row(n1_ref, h, sub, ls))
                e1 = packed_rows(first_key_row(e1_ref, h, sub, ls))
                gate = gate + jnp.where(r2_ref[ks, ls] < n1, e2_ref[ks, ls] * e1,
                                        jnp.zeros_like(gate))
            a = a_ref[es, ls]
            gelu = 0.5 * a * (1.0 + lax.erf(a * (2.0 ** -0.5)))
            w_ref[es, ls] = gelu.astype(BF16) * gate

    acc_ref[...] += _dot(vt_ref[...], w_ref[...])

    @pl.when(j == pl.num_programs(1) - 1)
    def _():
        ff = acc_ref[...].T
        y_ref[...] = _layer_norm(alpha * x_ref[...] + ff, g_ref[...], b_ref[...])


def _peer(x1, x1b, u, vt, n1, e1, r2, e2, ln_g, ln_b, alpha, tm=512, te=1024):
    n = x1.shape[0]
    n_experts = u.shape[0]
    assert (te // N_KEYS) % SUBLANES == 0
    row = lambda i, j: (i, 0)
    sel = pl.BlockSpec((PEER_HEADS * N_KEYS, tm), lambda i, j: (0, i),
                       pipeline_mode=pl.Buffered(1))
    vec = pl.BlockSpec((1, D_MODEL), lambda i, j: (0, 0))
    return pl.pallas_call(
        functools.partial(_peer_kernel, alpha=alpha),
        grid=(n // tm, n_experts // te),
        in_specs=[pl.BlockSpec((tm, D_MODEL), row),
                  pl.BlockSpec((te, D_MODEL), lambda i, j: (j, 0)),
                  pl.BlockSpec((D_MODEL, te), lambda i, j: (0, j)),
                  sel, sel, sel, sel,
                  pl.BlockSpec((tm, D_MODEL), row, pipeline_mode=pl.Buffered(1)), vec, vec],
        out_specs=pl.BlockSpec((tm, D_MODEL), row),
        out_shape=jax.ShapeDtypeStruct((n, D_MODEL), F32),
        scratch_shapes=[pltpu.VMEM((D_MODEL, tm), F32), pltpu.VMEM((te, tm), F32),
                        pltpu.VMEM((te, tm), BF16)],
        compiler_params=_params("parallel", "arbitrary"),
        name="peer_experts_ln",
    )(x1b, u, vt, n1, e1, r2, e2, x1, ln_g, ln_b)


def _encoder_layer(x, mem, p, alpha):
    batch, seq_len, _ = x.shape
    x2 = x.reshape(batch * seq_len, D_MODEL)
    q_a, k_a, v_a, q_b, k_b, v_b, q_m = _project(x2, p["w_qkv"], _rope_tables(seq_len), seq_len)
    o_a = _window_attention(q_a, k_a, v_a, p["a_sink"], batch, seq_len)
    o_b = _neighbourhood_attention(q_b, k_b, v_b, p["na_bias"], batch, seq_len)
    km, vm = _memory_kv(mem.reshape(-1, D_MODEL), p["w_mem_kv"])
    o_m = _memory_attention(q_m, km, vm, batch, seq_len)
    x1, x1b = _merge(x2, o_a, o_b, o_m, p["w_gate"], p["b_gate"], p["w_proj_a"], p["w_proj_b"],
                     p["w_proj_m"], p["w_out"], p["ln1_g"], p["ln1_b"], alpha)
    n1, e1, r2, e2 = _peer_prep(x1b, p["w_peer_q"], p["peer_keys1"], p["peer_keys2"])
    y = _peer(x1, x1b, p["peer_u"], p["peer_vt"], n1, e1, r2, e2, p["ln2_g"], p["ln2_b"], alpha)
    return y.reshape(batch, seq_len, D_MODEL)


def _layer_params(w_in, b_gate, a_sink, na_rpb, w_mem_kv, w_proj_a, w_proj_b, w_proj_m, w_out,
                  ln1_g, ln1_b, w_peer_q, peer_keys1, peer_keys2, peer_u, peer_v, ln2_g, ln2_b):
    row = lambda v: v.reshape(1, -1).astype(F32)
    return {
        "w_qkv": w_in[:, :QKV_W].astype(BF16),
        "w_gate": w_in[:, QKV_W:].astype(BF16),
        "b_gate": row(b_gate),
        "a_sink": a_sink.astype(F32),
        "na_bias": _na_bias_table(na_rpb),
        "w_mem_kv": w_mem_kv.astype(BF16),
        "w_proj_a": w_proj_a.astype(BF16),
        "w_proj_b": w_proj_b.astype(BF16),
        "w_proj_m": w_proj_m.astype(BF16),
        "w_out": w_out.astype(BF16),
        "ln1_g": row(ln1_g),
        "ln1_b": row(ln1_b),
        "w_peer_q": w_peer_q.astype(BF16),
        "peer_keys1": peer_keys1.astype(BF16),
        "peer_keys2": peer_keys2.astype(BF16),
        "peer_u": peer_u.astype(BF16),
        "peer_vt": peer_v.T.astype(BF16),
        "ln2_g": row(ln2_g),
        "ln2_b": row(ln2_b),
    }


def kernel(x_prompt, x_sample, mem_prompt, mem_sample, w_in, b_gate, a_sink, na_rpb, w_mem_kv,
           w_proj_a, w_proj_b, w_proj_m, w_out, ln1_g, ln1_b, w_peer_q, peer_keys1, peer_keys2,
           peer_u, peer_v, ln2_g, ln2_b):
    depth = w_in.shape[0]
    alpha = (2.0 * depth) ** 0.25
    y_prompt, y_sample = x_prompt, x_sample
    for l in range(depth):
        p = _layer_params(w_in[l], b_gate[l], a_sink[l], na_rpb[l], w_mem_kv[l], w_proj_a[l],
                          w_proj_b[l], w_proj_m[l], w_out[l], ln1_g[l], ln1_b[l], w_peer_q[l],
                          peer_keys1[l], peer_keys2[l], peer_u[l], peer_v[l], ln2_g[l], ln2_b[l])
        y_prompt = _encoder_layer(y_prompt, mem_prompt, p, alpha)
        y_sample = _encoder_layer(y_sample, mem_sample, p, alpha)
    return (y_prompt, y_sample)
```

```python
import functools

import jax
import jax.numpy as jnp
from jax import lax
from jax.experimental import pallas as pl
from jax.experimental.pallas import tpu as pltpu

D_MODEL = 2048
HEAD_DIM = 128
A_HEADS = 8
A_KV_HEADS = 2
A_GROUP = A_HEADS // A_KV_HEADS
WINDOW = 128
ROPE_THETA = 500000.0
ROPE_DIM = HEAD_DIM // 4
B_HEADS = 4
GRID_W = 64
NA_ROWS = 8
NA_COLS = 16
M_HEADS = 4
N_BRANCH = 3
A_Q_W = A_HEADS * HEAD_DIM
A_KV_W = A_KV_HEADS * HEAD_DIM
B_W = B_HEADS * HEAD_DIM
M_W = M_HEADS * HEAD_DIM
QKV_W = A_Q_W + 2 * A_KV_W + 3 * B_W + M_W
PEER_HEADS = 8
PEER_QDIM = 256
PEER_HALF = PEER_QDIM // 2
N_KEYS = 128
PEER_TOPK = 16
LN_EPS = 1e-5
NEG_INF = -1e30
ATTN_SCALE = HEAD_DIM ** -0.5

LANES = 128
SUBLANES = 8
VMEM_LIMIT = 56 * 1024 * 1024

BF16 = jnp.bfloat16
F32 = jnp.float32

_PROJ_PIECES = (("q_a", A_Q_W, True), ("k_a", A_KV_W, True), ("v_a", A_KV_W, False),
                ("q_b", B_W, False), ("k_b", B_W, False), ("v_b", B_W, False), ("q_m", M_W, False))
_COL_CHUNK = 512


def _dot(a, b):
    return jnp.dot(a, b, preferred_element_type=F32)


def _dot_nt(a, b):
    return lax.dot_general(a, b, (((1,), (1,)), ((), ())), preferred_element_type=F32)


def _params(*semantics):
    return pltpu.CompilerParams(dimension_semantics=semantics, vmem_limit_bytes=VMEM_LIMIT)


def _resident(shape):
    return pl.BlockSpec(shape, lambda *_: (0,) * len(shape), pipeline_mode=pl.Buffered(1))


def _layer_norm(h, g, b):
    mu = jnp.mean(h, axis=-1, keepdims=True)
    hc = h - mu
    var = jnp.mean(hc * hc, axis=-1, keepdims=True)
    return hc * lax.rsqrt(var + LN_EPS) * g + b


def _proj_kernel(x_ref, w_ref, cos_ref, sin_up_ref, sin_dn_ref, *out_refs):
    xb = x_ref[...].astype(BF16)
    off = 0
    for (_, width, rotary), o_ref in zip(_PROJ_PIECES, out_refs):
        for c0 in range(0, width, _COL_CHUNK):
            cw = min(_COL_CHUNK, width - c0)
            acc = _dot(xb, w_ref[:, off + c0:off + c0 + cw])
            if rotary:
                reps = cw // HEAD_DIM
                cos = jnp.concatenate([cos_ref[...]] * reps, axis=1)
                sin_up = jnp.concatenate([sin_up_ref[...]] * reps, axis=1)
                sin_dn = jnp.concatenate([sin_dn_ref[...]] * reps, axis=1)
                half = ROPE_DIM // 2
                acc = (acc * cos + pltpu.roll(acc, half, 1) * sin_up
                       + pltpu.roll(acc, cw - half, 1) * sin_dn)
            o_ref[:, c0:c0 + cw] = acc.astype(o_ref.dtype)
        off += width


def _rope_tables(seq_len):
    half = ROPE_DIM // 2
    inv_freq = ROPE_THETA ** (-jnp.arange(half, dtype=F32) * 2.0 / ROPE_DIM)
    ang = jnp.arange(seq_len, dtype=F32)[:, None] * inv_freq[None, :]
    cos, sin = jnp.cos(ang), jnp.sin(ang)
    zeros = jnp.zeros((seq_len, HEAD_DIM - ROPE_DIM), F32)
    zh = jnp.zeros((seq_len, half), F32)
    cos_t = jnp.concatenate([cos, cos, zeros + 1.0], axis=1)
    sin_up = jnp.concatenate([zh, sin, zeros], axis=1)
    sin_dn = jnp.concatenate([-sin, zh, zeros], axis=1)
    return cos_t, sin_up, sin_dn


def _project(x2, w_qkv, rope, seq_len, tm=512):
    n = x2.shape[0]
    blocks_per_seq = seq_len // tm
    tab_spec = pl.BlockSpec((tm, HEAD_DIM), lambda i: (i % blocks_per_seq, 0))
    return pl.pallas_call(
        _proj_kernel,
        grid=(n // tm,),
        in_specs=[pl.BlockSpec((tm, D_MODEL), lambda i: (i, 0)),
                  _resident((D_MODEL, QKV_W)), tab_spec, tab_spec, tab_spec],
        out_specs=[pl.BlockSpec((tm, w), lambda i: (i, 0)) for _, w, _ in _PROJ_PIECES],
        out_shape=[jax.ShapeDtypeStruct((n, w), BF16) for _, w, _ in _PROJ_PIECES],
        compiler_params=_params("parallel"),
        name="qkv_proj",
    )(x2, w_qkv, *rope)


_WIN_CHUNK = 512
_WIN_SUB = _WIN_CHUNK // WINDOW


def _win_attn_kernel(sink_ref, q_ref, kp_ref, kc_ref, kn_ref, vp_ref, vc_ref, vn_ref, o_ref):
    c = pl.program_id(1)
    rows = A_GROUP * WINDOW
    qi = lax.broadcasted_iota(jnp.int32, (rows, WINDOW), 0) % WINDOW
    kj = lax.broadcasted_iota(jnp.int32, (rows, WINDOW), 1)
    below = kj >= qi
    above = kj <= qi
    prev_pad = jnp.where(c > 0, 0.0, NEG_INF)
    next_pad = jnp.where(c < pl.num_programs(1) - 1, 0.0, NEG_INF)
    for g in range(A_KV_HEADS):
        hs = slice(g * HEAD_DIM, (g + 1) * HEAD_DIM)
        sink = jnp.concatenate(
            [jnp.full((WINDOW, 1), sink_ref[A_GROUP * g + u], F32) for u in range(A_GROUP)], axis=0)
        for sb in range(_WIN_SUB):
            rs = slice(sb * WINDOW, (sb + 1) * WINDOW)
            q = jnp.concatenate(
                [q_ref[rs, (A_GROUP * g + u) * HEAD_DIM:(A_GROUP * g + u + 1) * HEAD_DIM]
                 for u in range(A_GROUP)], axis=0)
            if sb == 0:
                k_prev, v_prev, pad_prev = kp_ref[:, hs], vp_ref[:, hs], prev_pad
            else:
                ps = slice((sb - 1) * WINDOW, sb * WINDOW)
                k_prev, v_prev, pad_prev = kc_ref[ps, hs], vc_ref[ps, hs], 0.0
            if sb == _WIN_SUB - 1:
                k_next, v_next, pad_next = kn_ref[:, hs], vn_ref[:, hs], next_pad
            else:
                ns = slice((sb + 1) * WINDOW, (sb + 2) * WINDOW)
                k_next, v_next, pad_next = kc_ref[ns, hs], vc_ref[ns, hs], 0.0
            s_prev = jnp.where(below, _dot_nt(q, k_prev) * ATTN_SCALE, NEG_INF) + pad_prev
            s_cur = _dot_nt(q, kc_ref[rs, hs]) * ATTN_SCALE
            s_next = jnp.where(above, _dot_nt(q, k_next) * ATTN_SCALE, NEG_INF) + pad_next
            m = jnp.max(jnp.maximum(jnp.maximum(s_prev, s_cur), s_next), axis=-1, keepdims=True)
            m = jnp.maximum(m, sink)
            p_prev, p_cur, p_next = jnp.exp(s_prev - m), jnp.exp(s_cur - m), jnp.exp(s_next - m)
            denom = jnp.sum(p_prev + p_cur + p_next, axis=-1, keepdims=True) + jnp.exp(sink - m)
            o = (_dot(p_prev.astype(BF16), v_prev) + _dot(p_cur.astype(BF16), vc_ref[rs, hs])
                 + _dot(p_next.astype(BF16), v_next)) / denom
            for u in range(A_GROUP):
                h = A_GROUP * g + u
                o_ref[rs, h * HEAD_DIM:(h + 1) * HEAD_DIM] = (
                    o[u * WINDOW:(u + 1) * WINDOW].astype(o_ref.dtype))


def _window_attention(q, k, v, sink, batch, seq_len):
    n = q.shape[0]
    chunks = seq_len // _WIN_CHUNK
    blocks = seq_len // WINDOW

    def cur(b, c):
        return (b * chunks + c, 0)

    def prev(b, c):
        return (b * blocks + jnp.maximum(c * _WIN_SUB - 1, 0), 0)

    def nxt(b, c):
        return (b * blocks + jnp.minimum((c + 1) * _WIN_SUB, blocks - 1), 0)

    def kv_specs():
        return [pl.BlockSpec((WINDOW, A_KV_W), prev), pl.BlockSpec((_WIN_CHUNK, A_KV_W), cur),
                pl.BlockSpec((WINDOW, A_KV_W), nxt)]

    return pl.pallas_call(
        _win_attn_kernel,
        grid=(batch, chunks),
        in_specs=[pl.BlockSpec(memory_space=pltpu.SMEM), pl.BlockSpec((_WIN_CHUNK, A_Q_W), cur),
                  *kv_specs(), *kv_specs()],
        out_specs=pl.BlockSpec((_WIN_CHUNK, A_Q_W), cur),
        out_shape=jax.ShapeDtypeStruct((n, A_Q_W), BF16),
        compiler_params=_params("parallel", "parallel"),
        name="window_attn",
    )(sink, q, k, k, k, v, v, v)


_NA_KEYS = NA_ROWS * GRID_W
_NA_CHUNK = NA_ROWS * GRID_W
_NA_HALO = _NA_CHUNK // 2
_NA_PAIR_KEYS = (NA_ROWS + 2) * GRID_W


def _na_bias_table(rpb):
    c = jnp.arange(GRID_W)
    cs = jnp.clip(c - NA_COLS // 2, 0, GRID_W - NA_COLS)
    col_ok = (c[None, :] >= cs[:, None]) & (c[None, :] < cs[:, None] + NA_COLS)
    dc = jnp.clip(c[None, :] - c[:, None], -(NA_COLS - 1), NA_COLS - 1) + NA_COLS - 1
    onehot = dc[:, :, None] == jnp.arange(2 * NA_COLS - 1)[None, None, :]
    by_col = jnp.sum(jnp.where(onehot[None, None], rpb.astype(F32)[:, :, None, None, :], 0.0), axis=-1)
    by_col = jnp.where(col_ok[None, None], by_col, NEG_INF)
    pats = [by_col[:, NA_ROWS - 1 - p:2 * NA_ROWS - 1 - p] for p in range(NA_ROWS)]
    bias = jnp.stack(pats, axis=0)
    bias = bias.transpose(0, 1, 3, 2, 4).reshape(NA_ROWS, B_HEADS, GRID_W, _NA_KEYS)
    spare = _NA_PAIR_KEYS - _NA_KEYS
    shifted = [jnp.pad(bias, ((0, 0), (0, 0), (0, 0), (s * GRID_W, spare - s * GRID_W)),
                       constant_values=NEG_INF) for s in range(2)]
    return jnp.stack(shifted, axis=1)


def _na_kernel(bias_ref, q_ref, kp_ref, kc_ref, kn_ref, vp_ref, vc_ref, vn_ref, o_ref,
               kbuf, vbuf, *, grid_rows):
    c = pl.program_id(1)
    kbuf[0:_NA_HALO] = kp_ref[...]
    kbuf[_NA_HALO:_NA_HALO + _NA_CHUNK] = kc_ref[...]
    kbuf[_NA_HALO + _NA_CHUNK:] = kn_ref[...]
    vbuf[0:_NA_HALO] = vp_ref[...]
    vbuf[_NA_HALO:_NA_HALO + _NA_CHUNK] = vc_ref[...]
    vbuf[_NA_HALO + _NA_CHUNK:] = vn_ref[...]
    halo_rows = _NA_HALO // GRID_W

    def window_start(r):
        return jnp.clip(r - NA_ROWS // 2, 0, grid_rows - NA_ROWS)

    for i in range(0, NA_ROWS, 2):
        r = c * NA_ROWS + i
        rs0, rs1 = window_start(r), window_start(r + 1)
        start = pl.multiple_of((rs0 - (c * NA_ROWS - halo_rows)) * GRID_W, GRID_W)
        qs = slice(i * GRID_W, (i + 2) * GRID_W)
        for h in range(B_HEADS):
            hs = slice(h * HEAD_DIM, (h + 1) * HEAD_DIM)
            kw = kbuf[pl.ds(start, _NA_PAIR_KEYS), hs]
            vw = vbuf[pl.ds(start, _NA_PAIR_KEYS), hs]
            bias = jnp.concatenate([bias_ref[r - rs0, 0, h], bias_ref[r + 1 - rs1, rs1 - rs0, h]], axis=0)
            s = _dot_nt(q_ref[qs, hs], kw) * ATTN_SCALE + bias
            m = jnp.max(s, axis=-1, keepdims=True)
            p = jnp.exp(s - m)
            denom = jnp.sum(p, axis=-1, keepdims=True)
            o_ref[qs, hs] = (_dot(p.astype(BF16), vw) / denom).astype(o_ref.dtype)


def _neighbourhood_attention(q, k, v, bias, batch, seq_len):
    n = q.shape[0]
    grid_rows = seq_len // GRID_W
    assert grid_rows >= 2 * NA_ROWS and seq_len % _NA_CHUNK == 0
    chunks = seq_len // _NA_CHUNK
    halos = seq_len // _NA_HALO

    def cur(b, c):
        return (b * chunks + c, 0)

    def prev(b, c):
        return (b * halos + jnp.maximum(2 * c - 1, 0), 0)

    def nxt(b, c):
        return (b * halos + jnp.minimum(2 * c + 2, halos - 1), 0)

    def kv_specs():
        return [pl.BlockSpec((_NA_HALO, B_W), prev), pl.BlockSpec((_NA_CHUNK, B_W), cur),
                pl.BlockSpec((_NA_HALO, B_W), nxt)]

    buf = pltpu.VMEM((_NA_CHUNK + 2 * _NA_HALO, B_W), BF16)
    return pl.pallas_call(
        functools.partial(_na_kernel, grid_rows=grid_rows),
        grid=(batch, chunks),
        in_specs=[_resident((NA_ROWS, 2, B_HEADS, GRID_W, _NA_PAIR_KEYS)),
                  pl.BlockSpec((_NA_CHUNK, B_W), cur), *kv_specs(), *kv_specs()],
        out_specs=pl.BlockSpec((_NA_CHUNK, B_W), cur),
        out_shape=jax.ShapeDtypeStruct((n, B_W), BF16),
        scratch_shapes=[buf, buf],
        compiler_params=_params("parallel", "parallel"),
        name="neighbourhood_attn",
    )(bias, q, k, k, k, v, v, v)


def _mem_kv_kernel(mem_ref, w_ref, k_ref, v_ref):
    kv = _dot(mem_ref[...].astype(BF16), w_ref[...])
    k_ref[...] = kv[:, :M_W].astype(k_ref.dtype)
    v_ref[...] = kv[:, M_W:].astype(v_ref.dtype)


def _memory_kv(mem2, w_mem_kv, tm=256):
    n = mem2.shape[0]
    out = jax.ShapeDtypeStruct((n, M_W), BF16)
    return pl.pallas_call(
        _mem_kv_kernel,
        grid=(n // tm,),
        in_specs=[pl.BlockSpec((tm, D_MODEL), lambda i: (i, 0)), _resident((D_MODEL, 2 * M_W))],
        out_specs=[pl.BlockSpec((tm, M_W), lambda i: (i, 0))] * 2,
        out_shape=[out, out],
        compiler_params=_params("parallel"),
        name="memory_kv",
    )(mem2, w_mem_kv)


def _mem_attn_kernel(q_ref, k_ref, v_ref, o_ref):
    for h in range(M_HEADS):
        hs = slice(h * HEAD_DIM, (h + 1) * HEAD_DIM)
        s = _dot_nt(q_ref[:, hs], k_ref[:, hs]) * ATTN_SCALE
        m = jnp.max(s, axis=-1, keepdims=True)
        p = jnp.exp(s - m)
        denom = jnp.sum(p, axis=-1, keepdims=True)
        o_ref[:, hs] = (_dot(p.astype(BF16), v_ref[:, hs]) / denom).astype(o_ref.dtype)


def _memory_attention(q, km, vm, batch, seq_len, tm=512):
    n = q.shape[0]
    n_mem = km.shape[0] // batch
    chunks = seq_len // tm
    return pl.pallas_call(
        _mem_attn_kernel,
        grid=(batch, chunks),
        in_specs=[pl.BlockSpec((tm, M_W), lambda b, c: (b * chunks + c, 0)),
                  pl.BlockSpec((n_mem, M_W), lambda b, c: (b, 0)),
                  pl.BlockSpec((n_mem, M_W), lambda b, c: (b, 0))],
        out_specs=pl.BlockSpec((tm, M_W), lambda b, c: (b * chunks + c, 0)),
        out_shape=jax.ShapeDtypeStruct((n, M_W), BF16),
        compiler_params=_params("parallel", "parallel"),
        name="memory_attn",
    )(q, km, vm)


_MERGE_COLS = 512


def _merge_kernel(x_ref, oa_ref, ob_ref, om_ref, wg0_ref, wg1_ref, wg2_ref, bg0_ref, bg1_ref,
                  bg2_ref, wa_ref, wb_ref, wm_ref, wo_ref, g_ref, b_ref, y_ref, yb_ref,
                  xb_ref, acc_ref, *, alpha):
    j = pl.program_id(1)

    @pl.when(j == 0)
    def _():
        xb_ref[...] = x_ref[...].astype(BF16)
        acc_ref[...] = jnp.zeros_like(acc_ref)

    xb = xb_ref[...]
    merged = (jax.nn.sigmoid(_dot(xb, wg0_ref[...]) + bg0_ref[...]) * _dot(oa_ref[...], wa_ref[...])
              + jax.nn.sigmoid(_dot(xb, wg1_ref[...]) + bg1_ref[...]) * _dot(ob_ref[...], wb_ref[...])
              + jax.nn.sigmoid(_dot(xb, wg2_ref[...]) + bg2_ref[...]) * _dot(om_ref[...], wm_ref[...]))
    acc_ref[...] += _dot(merged.astype(BF16), wo_ref[...])

    @pl.when(j == pl.num_programs(1) - 1)
    def _():
        y = _layer_norm(alpha * x_ref[...] + acc_ref[...], g_ref[...], b_ref[...])
        y_ref[...] = y
        yb_ref[...] = y.astype(BF16)


def _merge(x2, o_a, o_b, o_m, w_gate, b_gate, w_pa, w_pb, w_pm, w_out, ln_g, ln_b, alpha, tm=512):
    n = x2.shape[0]
    steps = D_MODEL // _MERGE_COLS
    row = lambda i, j: (i, 0)

    def gate_w(branch):
        return pl.BlockSpec((D_MODEL, _MERGE_COLS), lambda i, j: (0, branch * steps + j))

    def gate_b(branch):
        return pl.BlockSpec((1, _MERGE_COLS), lambda i, j: (0, branch * steps + j))

    col = lambda i, j: (0, j)
    vec = pl.BlockSpec((1, D_MODEL), lambda i, j: (0, 0))

    def once(shape):
        return pl.BlockSpec(shape, row, pipeline_mode=pl.Buffered(1))

    return pl.pallas_call(
        functools.partial(_merge_kernel, alpha=alpha),
        grid=(n // tm, steps),
        in_specs=[pl.BlockSpec((tm, D_MODEL), row), once((tm, A_Q_W)), once((tm, B_W)),
                  once((tm, M_W)),
                  gate_w(0), gate_w(1), gate_w(2), gate_b(0), gate_b(1), gate_b(2),
                  pl.BlockSpec((A_Q_W, _MERGE_COLS), col), pl.BlockSpec((B_W, _MERGE_COLS), col),
                  pl.BlockSpec((M_W, _MERGE_COLS), col),
                  pl.BlockSpec((_MERGE_COLS, D_MODEL), lambda i, j: (j, 0)), vec, vec],
        out_specs=[pl.BlockSpec((tm, D_MODEL), row), pl.BlockSpec((tm, D_MODEL), row)],
        out_shape=[jax.ShapeDtypeStruct((n, D_MODEL), F32), jax.ShapeDtypeStruct((n, D_MODEL), BF16)],
        scratch_shapes=[pltpu.VMEM((tm, D_MODEL), BF16), pltpu.VMEM((tm, D_MODEL), F32)],
        compiler_params=_params("parallel", "arbitrary"),
        name="merge_out_ln",
    )(x2, o_a, o_b, o_m, w_gate, w_gate, w_gate, b_gate, b_gate, b_gate, w_pa, w_pb, w_pm, w_out,
      ln_g, ln_b)


def _take_top(s, count, row_id):
    order = jnp.full(s.shape, float(count), F32)
    vals, ids = [], []
    for a in range(count):
        m = jnp.max(s, axis=0, keepdims=True)
        rid = jnp.min(jnp.where(s == m, row_id, _NO_ROW), axis=0, keepdims=True)
        hit = row_id == rid
        order = jnp.where(hit, float(a), order)
        s = jnp.where(hit, -jnp.inf, s)
        vals.append(m)
        ids.append(rid)
    return vals, ids, order


def _sorting_network(n):
    pairs = []
    p = 1
    while p < n:
        k = p
        while k >= 1:
            for j in range(k % p, n - k, 2 * k):
                for i in range(min(k, n - j - k)):
                    if (i + j) // (2 * p) == (i + j + k) // (2 * p):
                        pairs.append((i + j, i + j + k))
            k //= 2
        p *= 2
    return pairs


def _top_sorted(s):
    assert s.shape[0] == PEER_TOPK * SUBLANES
    v = [s[g * SUBLANES:(g + 1) * SUBLANES] for g in range(PEER_TOPK)]

    def exchange(i, j):
        v[i], v[j] = jnp.maximum(v[i], v[j]), jnp.minimum(v[i], v[j])

    for i, j in _sorting_network(PEER_TOPK):
        exchange(i, j)
    shift = SUBLANES // 2
    while shift >= 1:
        other = [pltpu.roll(x, shift, 0) for x in v]
        v = [jnp.maximum(v[i], other[PEER_TOPK - 1 - i]) for i in range(PEER_TOPK)]
        d = PEER_TOPK // 2
        while d >= 1:
            for i in range(PEER_TOPK):
                if not i & d:
                    exchange(i, i + d)
            d //= 2
        shift //= 2
    return v


def _count_greater(t, x):
    assert len(t) == 16
    b3 = t[7] > x
    b2 = jnp.where(b3, t[11], t[3]) > x
    b1 = jnp.where(b3, jnp.where(b2, t[13], t[9]), jnp.where(b2, t[5], t[1])) > x
    upper = jnp.where(b2, jnp.where(b1, t[14], t[12]), jnp.where(b1, t[10], t[8]))
    lower = jnp.where(b2, jnp.where(b1, t[6], t[4]), jnp.where(b1, t[2], t[0]))
    b0 = jnp.where(b3, upper, lower) > x
    count = (jnp.where(b3, 8.0, 0.0) + jnp.where(b2, 4.0, 0.0) + jnp.where(b1, 2.0, 0.0)
             + jnp.where(b0, 1.0, 0.0))
    return jnp.where(t[15] > x, 16.0, count)


def _top_untied(s):
    t = _top_sorted(s)
    groups = [s[g * SUBLANES:(g + 1) * SUBLANES] for g in range(s.shape[0] // SUBLANES)]
    order = jnp.concatenate([_count_greater(t, x) for x in groups], axis=0)
    tied = jnp.zeros_like(t[0])
    for a in range(PEER_TOPK - 1):
        tied = jnp.maximum(tied, jnp.where(t[a] == t[a + 1], 1.0, 0.0))
    reach = jnp.zeros_like(t[0])
    for x in groups:
        reach = reach + jnp.where(x >= t[PEER_TOPK - 1], 1.0, 0.0)
    shift = SUBLANES // 2
    while shift >= 1:
        reach = reach + pltpu.roll(reach, shift, 0)
        shift //= 2
    tied = jnp.maximum(tied, jnp.where(reach != float(PEER_TOPK), 1.0, 0.0))
    return [x[0:1] for x in t], order, tied


_NO_ROW = 1e9
_CAND_COUNTS = tuple(PEER_TOPK // (a + 1) for a in range(PEER_TOPK))
_CAND_ROWS = -(-sum(_CAND_COUNTS) // SUBLANES) * SUBLANES


_PREP_LANES = 2 * LANES


def _candidate_ids():
    ids = [a * PEER_TOPK + b for a, nb in enumerate(_CAND_COUNTS) for b in range(nb)]
    ids += [_NO_ROW] * (_CAND_ROWS - len(ids))
    return jnp.broadcast_to(jnp.asarray(ids, F32)[:, None], (_CAND_ROWS, _PREP_LANES))


def _peer_prep_kernel(xb_ref, wq_ref, k1_ref, k2_ref, cid_ref, n1_ref, e1_ref, r2_ref, e2_ref,
                      st_ref):
    tm = xb_ref.shape[0]
    q = _dot(xb_ref[...], wq_ref[...]).astype(BF16)
    for h in range(PEER_HEADS):
        q1 = q[:, h * PEER_QDIM:h * PEER_QDIM + PEER_HALF]
        q2 = q[:, h * PEER_QDIM + PEER_HALF:(h + 1) * PEER_QDIM]
        st_ref[2 * h] = _dot_nt(k1_ref[...], q1)
        st_ref[2 * h + 1] = _dot_nt(k2_ref[...], q2)
    width = _PREP_LANES
    lane_chunks = tm // width

    def body(it, carry):
        h = it // lane_chunks
        lanes = pl.ds(pl.multiple_of((it % lane_chunks) * width, width), width)
        s1 = st_ref[2 * h, :, lanes]
        s2 = st_ref[2 * h + 1, :, lanes]
        t1, order1, tied1 = _top_untied(s1)
        t2, order2, tied2 = _top_untied(s2)

        def with_ties():
            key_id = lax.broadcasted_iota(jnp.int32, (N_KEYS, width), 0).astype(F32)
            v1, _, o1 = _take_top(s1, PEER_TOPK, key_id)
            v2, _, o2 = _take_top(s2, PEER_TOPK, key_id)
            return tuple(v1), o1, tuple(v2), o2

        t1, order1, t2, order2 = lax.cond(jnp.max(jnp.maximum(tied1, tied2)) > 0.0, with_ties,
                                          lambda: (tuple(t1), order1, tuple(t2), order2))
        t2_all = jnp.concatenate(t2, axis=0)
        pad = jnp.full((_CAND_ROWS - sum(_CAND_COUNTS), width), -jnp.inf, F32)
        cand = jnp.concatenate(
            [t1[a] + t2_all[:nb] for a, nb in enumerate(_CAND_COUNTS)] + [pad], axis=0)
        top, cid, _ = _take_top(cand, PEER_TOPK, cid_ref[...])
        a_id = lax.broadcasted_iota(jnp.int32, (PEER_TOPK, width), 0).astype(F32)
        taken = jnp.zeros((PEER_TOPK, width), F32)
        z = jnp.zeros((1, width), F32)
        for k in range(PEER_TOPK):
            taken = taken + jnp.where(a_id == jnp.floor(cid[k] * (1.0 / PEER_TOPK)), 1.0, 0.0)
            z = z + jnp.exp(top[k] - top[0])
        n1 = jnp.zeros((N_KEYS, width), F32)
        for a in range(PEER_TOPK):
            n1 = jnp.where(order1 == float(a), taken[a:a + 1, :], n1)
        keys = pl.ds(pl.multiple_of(h * N_KEYS, N_KEYS), N_KEYS)
        n1_ref[keys, lanes] = n1
        e1_ref[keys, lanes] = jnp.exp(s1 - t1[0])
        r2_ref[keys, lanes] = order2.astype(r2_ref.dtype)
        e2_ref[keys, lanes] = (jnp.exp(s2 - t2[0]) / z).astype(e2_ref.dtype)
        return carry

    lax.fori_loop(0, PEER_HEADS * lane_chunks, body, 0)


def _peer_prep(x1b, w_q, keys1, keys2, tm=512):
    n = x1b.shape[0]
    by_key1 = jax.ShapeDtypeStruct((PEER_HEADS * N_KEYS, n), F32)
    by_key2 = jax.ShapeDtypeStruct((PEER_HEADS * N_KEYS, n), BF16)
    spec = pl.BlockSpec((PEER_HEADS * N_KEYS, tm), lambda i: (0, i))
    return pl.pallas_call(
        _peer_prep_kernel,
        grid=(n // tm,),
        in_specs=[pl.BlockSpec((tm, D_MODEL), lambda i: (i, 0)),
                  _resident((D_MODEL, PEER_HEADS * PEER_QDIM)),
                  _resident((N_KEYS, PEER_HALF)), _resident((N_KEYS, PEER_HALF)),
                  _resident((_CAND_ROWS, _PREP_LANES))],
        out_specs=[spec] * 4,
        out_shape=[by_key1, by_key1, by_key2, by_key2],
        scratch_shapes=[pltpu.VMEM((2 * PEER_HEADS, N_KEYS, tm), F32)],
        compiler_params=_params("parallel"),
        name="peer_retrieve",
    )(x1b, w_q, keys1, keys2, _candidate_ids())


def _peer_kernel(xb_ref, u_ref, vt_ref, n1_ref, e1_ref, r2_ref, e2_ref, x_ref, g_ref, b_ref,
                 y_ref, acc_ref, a_ref, w_ref, *, alpha):
    j = pl.program_id(1)
    te, tm = a_ref.shape

    @pl.when(j == 0)
    def _():
        acc_ref[...] = jnp.zeros_like(acc_ref)

    a_ref[...] = _dot_nt(u_ref[...], xb_ref[...])
    keys_per_step = te // N_KEYS
    first_key = pl.multiple_of(j * keys_per_step, SUBLANES)

    def first_key_row(ref, h, sub, ls):
        return ref[pl.ds(h * N_KEYS + first_key, SUBLANES), ls][sub:sub + 1]

    chunk = 2 * LANES

    def packed_rows(row):
        tile_rows = 2 * SUBLANES
        tile = jnp.broadcast_to(row, (tile_rows, chunk)).astype(BF16)
        return jnp.concatenate([tile] * (N_KEYS // tile_rows), axis=0)

    for sub in range(keys_per_step):
        es = slice(sub * N_KEYS, (sub + 1) * N_KEYS)
        for tc in range(tm // chunk):
            ls = slice(tc * chunk, (tc + 1) * chunk)
            gate = jnp.zeros((N_KEYS, chunk), BF16)
            for h in range(PEER_HEADS):
                ks = slice(h * N_KEYS, (h + 1) * N_KEYS)
                n1 = packed_rows(first_key_row(n1_ref, h, sub, ls))
                e1 = packed_rows(first_key_row(e1_ref, h, sub, ls))
                gate = gate + jnp.where(r2_ref[ks, ls] < n1, e2_ref[ks, ls] * e1,
                                        jnp.zeros_like(gate))
            a = a_ref[es, ls]
            gelu = 0.5 * a * (1.0 + lax.erf(a * (2.0 ** -0.5)))
            w_ref[es, ls] = gelu.astype(BF16) * gate

    acc_ref[...] += _dot(vt_ref[...], w_ref[...])

    @pl.when(j == pl.num_programs(1) - 1)
    def _():
        ff = acc_ref[...].T
        y_ref[...] = _layer_norm(alpha * x_ref[...] + ff, g_ref[...], b_ref[...])


def _peer(x1, x1b, u, vt, n1, e1, r2, e2, ln_g, ln_b, alpha, tm=512, te=1024):
    n = x1.shape[0]
    n_experts = u.shape[0]
    assert (te // N_KEYS) % SUBLANES == 0
    row = lambda i, j: (i, 0)
    sel = pl.BlockSpec((PEER_HEADS * N_KEYS, tm), lambda i, j: (0, i),
                       pipeline_mode=pl.Buffered(1))
    vec = pl.BlockSpec((1, D_MODEL), lambda i, j: (0, 0))
    return pl.pallas_call(
        functools.partial(_peer_kernel, alpha=alpha),
        grid=(n // tm, n_experts // te),
        in_specs=[pl.BlockSpec((tm, D_MODEL), row),
                  pl.BlockSpec((te, D_MODEL), lambda i, j: (j, 0)),
                  pl.BlockSpec((D_MODEL, te), lambda i, j: (0, j)),
                  sel, sel, sel, sel,
                  pl.BlockSpec((tm, D_MODEL), row, pipeline_mode=pl.Buffered(1)), vec, vec],
        out_specs=pl.BlockSpec((tm, D_MODEL), row),
        out_shape=jax.ShapeDtypeStruct((n, D_MODEL), F32),
        scratch_shapes=[pltpu.VMEM((D_MODEL, tm), F32), pltpu.VMEM((te, tm), F32),
                        pltpu.VMEM((te, tm), BF16)],
        compiler_params=_params("parallel", "arbitrary"),
        name="peer_experts_ln",
    )(x1b, u, vt, n1, e1, r2, e2, x1, ln_g, ln_b)


def _encoder_layer(x, mem, p, alpha):
    batch, seq_len, _ = x.shape
    x2 = x.reshape(batch * seq_len, D_MODEL)
    q_a, k_a, v_a, q_b, k_b, v_b, q_m = _project(x2, p["w_qkv"], _rope_tables(seq_len), seq_len)
    o_a = _window_attention(q_a, k_a, v_a, p["a_sink"], batch, seq_len)
    o_b = _neighbourhood_attention(q_b, k_b, v_b, p["na_bias"], batch, seq_len)
    km, vm = _memory_kv(mem.reshape(-1, D_MODEL), p["w_mem_kv"])
    o_m = _memory_attention(q_m, km, vm, batch, seq_len)
    x1, x1b = _merge(x2, o_a, o_b, o_m, p["w_gate"], p["b_gate"], p["w_proj_a"], p["w_proj_b"],
                     p["w_proj_m"], p["w_out"], p["ln1_g"], p["ln1_b"], alpha)
    n1, e1, r2, e2 = _peer_prep(x1b, p["w_peer_q"], p["peer_keys1"], p["peer_keys2"])
    y = _peer(x1, x1b, p["peer_u"], p["peer_vt"], n1, e1, r2, e2, p["ln2_g"], p["ln2_b"], alpha)
    return y.reshape(batch, seq_len, D_MODEL)


def _layer_params(w_in, b_gate, a_sink, na_rpb, w_mem_kv, w_proj_a, w_proj_b, w_proj_m, w_out,
                  ln1_g, ln1_b, w_peer_q, peer_keys1, peer_keys2, peer_u, peer_v, ln2_g, ln2_b):
    row = lambda v: v.reshape(1, -1).astype(F32)
    return {
        "w_qkv": w_in[:, :QKV_W].astype(BF16),
        "w_gate": w_in[:, QKV_W:].astype(BF16),
        "b_gate": row(b_gate),
        "a_sink": a_sink.astype(F32),
        "na_bias": _na_bias_table(na_rpb),
        "w_mem_kv": w_mem_kv.astype(BF16),
        "w_proj_a": w_proj_a.astype(BF16),
        "w_proj_b": w_proj_b.astype(BF16),
        "w_proj_m": w_proj_m.astype(BF16),
        "w_out": w_out.astype(BF16),
        "ln1_g": row(ln1_g),
        "ln1_b": row(ln1_b),
        "w_peer_q": w_peer_q.astype(BF16),
        "peer_keys1": peer_keys1.astype(BF16),
        "peer_keys2": peer_keys2.astype(BF16),
        "peer_u": peer_u.astype(BF16),
        "peer_vt": peer_v.T.astype(BF16),
        "ln2_g": row(ln2_g),
        "ln2_b": row(ln2_b),
    }


def kernel(x_prompt, x_sample, mem_prompt, mem_sample, w_in, b_gate, a_sink, na_rpb, w_mem_kv,
           w_proj_a, w_proj_b, w_proj_m, w_out, ln1_g, ln1_b, w_peer_q, peer_keys1, peer_keys2,
           peer_u, peer_v, ln2_g, ln2_b):
    depth = w_in.shape[0]
    alpha = (2.0 * depth) ** 0.25
    y_prompt, y_sample = x_prompt, x_sample
    for l in range(depth):
        p = _layer_params(w_in[l], b_gate[l], a_sink[l], na_rpb[l], w_mem_kv[l], w_proj_a[l],
                          w_proj_b[l], w_proj_m[l], w_out[l], ln1_g[l], ln1_b[l], w_peer_q[l],
                          peer_keys1[l], peer_keys2[l], peer_u[l], peer_v[l], ln2_g[l], ln2_b[l])
        y_prompt = _encoder_layer(y_prompt, mem_prompt, p, alpha)
        y_sample = _encoder_layer(y_sample, mem_sample, p, alpha)
    return (y_prompt, y_sample)
```

```python
import functools

import jax
import jax.numpy as jnp
from jax import lax
from jax.experimental import pallas as pl
from jax.experimental.pallas import tpu as pltpu

D_MODEL = 2048
HEAD_DIM = 128
A_HEADS = 8
A_KV_HEADS = 2
A_GROUP = A_HEADS // A_KV_HEADS
WINDOW = 128
ROPE_THETA = 500000.0
ROPE_DIM = HEAD_DIM // 4
B_HEADS = 4
GRID_W = 64
NA_ROWS = 8
NA_COLS = 16
M_HEADS = 4
A_Q_W = A_HEADS * HEAD_DIM
A_KV_W = A_KV_HEADS * HEAD_DIM
B_W = B_HEADS * HEAD_DIM
M_W = M_HEADS * HEAD_DIM
QKV_W = A_Q_W + 2 * A_KV_W + 3 * B_W + M_W
PEER_HEADS = 8
PEER_QDIM = 256
PEER_HALF = PEER_QDIM // 2
N_KEYS = 128
PEER_TOPK = 16
LN_EPS = 1e-5
NEG_INF = -1e30
ATTN_SCALE = HEAD_DIM ** -0.5

LANES = 128
SUBLANES = 8
VMEM_LIMIT = 56 * 1024 * 1024
TOKEN_BLOCK = 512
PEER_TILE = 1024

BF16 = jnp.bfloat16
F32 = jnp.float32

_PROJ_PIECES = (("q_a", A_Q_W, True), ("k_a", A_KV_W, True), ("v_a", A_KV_W, False),
                ("q_b", B_W, False), ("k_b", B_W, False), ("v_b", B_W, False), ("q_m", M_W, False))
_COL_CHUNK = 512


def _dot(a, b):
    return jnp.dot(a, b, preferred_element_type=F32)


def _dot_nt(a, b):
    return lax.dot_general(a, b, (((1,), (1,)), ((), ())), preferred_element_type=F32)


def _params(*semantics):
    return pltpu.CompilerParams(dimension_semantics=semantics, vmem_limit_bytes=VMEM_LIMIT)


def _resident(shape):
    return pl.BlockSpec(shape, lambda *_: (0,) * len(shape), pipeline_mode=pl.Buffered(1))


def _layer_norm(h, g, b):
    mu = jnp.mean(h, axis=-1, keepdims=True)
    hc = h - mu
    var = jnp.mean(hc * hc, axis=-1, keepdims=True)
    return hc * lax.rsqrt(var + LN_EPS) * g + b


def _proj_kernel(x_ref, w_ref, cos_ref, sin_up_ref, sin_dn_ref, *out_refs):
    xb = x_ref[...].astype(BF16)
    off = 0
    for (_, width, rotary), o_ref in zip(_PROJ_PIECES, out_refs):
        for c0 in range(0, width, _COL_CHUNK):
            cw = min(_COL_CHUNK, width - c0)
            acc = _dot(xb, w_ref[:, off + c0:off + c0 + cw])
            if rotary:
                reps = cw // HEAD_DIM
                cos = jnp.concatenate([cos_ref[...]] * reps, axis=1)
                sin_up = jnp.concatenate([sin_up_ref[...]] * reps, axis=1)
                sin_dn = jnp.concatenate([sin_dn_ref[...]] * reps, axis=1)
                half = ROPE_DIM // 2
                acc = (acc * cos + pltpu.roll(acc, half, 1) * sin_up
                       + pltpu.roll(acc, cw - half, 1) * sin_dn)
            o_ref[:, c0:c0 + cw] = acc.astype(o_ref.dtype)
        off += width


def _rope_tables(seq_len):
    half = ROPE_DIM // 2
    inv_freq = ROPE_THETA ** (-jnp.arange(half, dtype=F32) * 2.0 / ROPE_DIM)
    ang = jnp.arange(seq_len, dtype=F32)[:, None] * inv_freq[None, :]
    cos, sin = jnp.cos(ang), jnp.sin(ang)
    zeros = jnp.zeros((seq_len, HEAD_DIM - ROPE_DIM), F32)
    zh = jnp.zeros((seq_len, half), F32)
    cos_t = jnp.concatenate([cos, cos, zeros + 1.0], axis=1)
    sin_up = jnp.concatenate([zh, sin, zeros], axis=1)
    sin_dn = jnp.concatenate([-sin, zh, zeros], axis=1)
    return cos_t, sin_up, sin_dn


def _project(x2, w_qkv, rope, seq_len, tm=TOKEN_BLOCK):
    n = x2.shape[0]
    blocks_per_seq = seq_len // tm
    tab_spec = pl.BlockSpec((tm, HEAD_DIM), lambda i: (i % blocks_per_seq, 0))
    return pl.pallas_call(
        _proj_kernel,
        grid=(n // tm,),
        in_specs=[pl.BlockSpec((tm, D_MODEL), lambda i: (i, 0)),
                  _resident((D_MODEL, QKV_W)), tab_spec, tab_spec, tab_spec],
        out_specs=[pl.BlockSpec((tm, w), lambda i: (i, 0)) for _, w, _ in _PROJ_PIECES],
        out_shape=[jax.ShapeDtypeStruct((n, w), BF16) for _, w, _ in _PROJ_PIECES],
        compiler_params=_params("parallel"),
        name="qkv_proj",
    )(x2, w_qkv, *rope)


_WIN_CHUNK = 2048
_WIN_SUB = _WIN_CHUNK // WINDOW


def _win_attn_kernel(sink_ref, q_ref, kp_ref, kc_ref, kn_ref, vp_ref, vc_ref, vn_ref, o_ref):
    c = pl.program_id(1)
    rows = A_GROUP * WINDOW
    qi = lax.broadcasted_iota(jnp.int32, (rows, WINDOW), 0) % WINDOW
    kj = lax.broadcasted_iota(jnp.int32, (rows, WINDOW), 1)
    below = kj >= qi
    above = kj <= qi
    prev_pad = jnp.where(c > 0, 0.0, NEG_INF)
    next_pad = jnp.where(c < pl.num_programs(1) - 1, 0.0, NEG_INF)
    for g in range(A_KV_HEADS):
        hs = slice(g * HEAD_DIM, (g + 1) * HEAD_DIM)
        sink = jnp.concatenate(
            [jnp.full((WINDOW, 1), sink_ref[A_GROUP * g + u], F32) for u in range(A_GROUP)], axis=0)
        for sb in range(_WIN_SUB):
            rs = slice(sb * WINDOW, (sb + 1) * WINDOW)
            q = jnp.concatenate(
                [q_ref[rs, (A_GROUP * g + u) * HEAD_DIM:(A_GROUP * g + u + 1) * HEAD_DIM]
                 for u in range(A_GROUP)], axis=0)
            if sb == 0:
                k_prev, v_prev, pad_prev = kp_ref[:, hs], vp_ref[:, hs], prev_pad
            else:
                ps = slice((sb - 1) * WINDOW, sb * WINDOW)
                k_prev, v_prev, pad_prev = kc_ref[ps, hs], vc_ref[ps, hs], 0.0
            if sb == _WIN_SUB - 1:
                k_next, v_next, pad_next = kn_ref[:, hs], vn_ref[:, hs], next_pad
            else:
                ns = slice((sb + 1) * WINDOW, (sb + 2) * WINDOW)
                k_next, v_next, pad_next = kc_ref[ns, hs], vc_ref[ns, hs], 0.0
            s_prev = jnp.where(below, _dot_nt(q, k_prev) * ATTN_SCALE, NEG_INF) + pad_prev
            s_cur = _dot_nt(q, kc_ref[rs, hs]) * ATTN_SCALE
            s_next = jnp.where(above, _dot_nt(q, k_next) * ATTN_SCALE, NEG_INF) + pad_next
            m = jnp.max(jnp.maximum(jnp.maximum(s_prev, s_cur), s_next), axis=-1, keepdims=True)
            m = jnp.maximum(m, sink)
            p_prev, p_cur, p_next = jnp.exp(s_prev - m), jnp.exp(s_cur - m), jnp.exp(s_next - m)
            denom = jnp.sum(p_prev + p_cur + p_next, axis=-1, keepdims=True) + jnp.exp(sink - m)
            o = (_dot(p_prev.astype(BF16), v_prev) + _dot(p_cur.astype(BF16), vc_ref[rs, hs])
                 + _dot(p_next.astype(BF16), v_next)) / denom
            for u in range(A_GROUP):
                h = A_GROUP * g + u
                o_ref[rs, h * HEAD_DIM:(h + 1) * HEAD_DIM] = (
                    o[u * WINDOW:(u + 1) * WINDOW].astype(o_ref.dtype))


def _window_attention(q, k, v, sink, batch, seq_len):
    n = q.shape[0]
    chunks = seq_len // _WIN_CHUNK
    blocks = seq_len // WINDOW

    def cur(b, c):
        return (b * chunks + c, 0)

    def prev(b, c):
        return (b * blocks + jnp.maximum(c * _WIN_SUB - 1, 0), 0)

    def nxt(b, c):
        return (b * blocks + jnp.minimum((c + 1) * _WIN_SUB, blocks - 1), 0)

    def kv_specs():
        return [pl.BlockSpec((WINDOW, A_KV_W), prev), pl.BlockSpec((_WIN_CHUNK, A_KV_W), cur),
                pl.BlockSpec((WINDOW, A_KV_W), nxt)]

    return pl.pallas_call(
        _win_attn_kernel,
        grid=(batch, chunks),
        in_specs=[pl.BlockSpec(memory_space=pltpu.SMEM), pl.BlockSpec((_WIN_CHUNK, A_Q_W), cur),
                  *kv_specs(), *kv_specs()],
        out_specs=pl.BlockSpec((_WIN_CHUNK, A_Q_W), cur),
        out_shape=jax.ShapeDtypeStruct((n, A_Q_W), BF16),
        compiler_params=_params("parallel", "parallel"),
        name="window_attn",
    )(sink, q, k, k, k, v, v, v)


_NA_KEYS = NA_ROWS * GRID_W
_NA_CHUNK = NA_ROWS * GRID_W
_NA_HALO = _NA_CHUNK // 2
_NA_PAIR_KEYS = (NA_ROWS + 2) * GRID_W


def _na_bias_table(rpb):
    c = jnp.arange(GRID_W)
    cs = jnp.clip(c - NA_COLS // 2, 0, GRID_W - NA_COLS)
    col_ok = (c[None, :] >= cs[:, None]) & (c[None, :] < cs[:, None] + NA_COLS)
    dc = jnp.clip(c[None, :] - c[:, None], -(NA_COLS - 1), NA_COLS - 1) + NA_COLS - 1
    onehot = dc[:, :, None] == jnp.arange(2 * NA_COLS - 1)[None, None, :]
    by_col = jnp.sum(jnp.where(onehot[None, None], rpb.astype(F32)[:, :, None, None, :], 0.0), axis=-1)
    by_col = jnp.where(col_ok[None, None], by_col, NEG_INF)
    pats = [by_col[:, NA_ROWS - 1 - p:2 * NA_ROWS - 1 - p] for p in range(NA_ROWS)]
    bias = jnp.stack(pats, axis=0)
    bias = bias.transpose(0, 1, 3, 2, 4).reshape(NA_ROWS, B_HEADS, GRID_W, _NA_KEYS)
    spare = _NA_PAIR_KEYS - _NA_KEYS
    shifted = [jnp.pad(bias, ((0, 0), (0, 0), (0, 0), (s * GRID_W, spare - s * GRID_W)),
                       constant_values=NEG_INF) for s in range(2)]
    return jnp.stack(shifted, axis=1)


def _na_kernel(bias_ref, q_ref, kp_ref, kc_ref, kn_ref, vp_ref, vc_ref, vn_ref, o_ref,
               kbuf, vbuf, *, grid_rows):
    c = pl.program_id(1)
    kbuf[0:_NA_HALO] = kp_ref[...]
    kbuf[_NA_HALO:_NA_HALO + _NA_CHUNK] = kc_ref[...]
    kbuf[_NA_HALO + _NA_CHUNK:] = kn_ref[...]
    vbuf[0:_NA_HALO] = vp_ref[...]
    vbuf[_NA_HALO:_NA_HALO + _NA_CHUNK] = vc_ref[...]
    vbuf[_NA_HALO + _NA_CHUNK:] = vn_ref[...]
    halo_rows = _NA_HALO // GRID_W

    def window_start(r):
        return jnp.clip(r - NA_ROWS // 2, 0, grid_rows - NA_ROWS)

    for i in range(0, NA_ROWS, 2):
        r = c * NA_ROWS + i
        rs0, rs1 = window_start(r), window_start(r + 1)
        start = pl.multiple_of((rs0 - (c * NA_ROWS - halo_rows)) * GRID_W, GRID_W)
        qs = slice(i * GRID_W, (i + 2) * GRID_W)
        for h in range(B_HEADS):
            hs = slice(h * HEAD_DIM, (h + 1) * HEAD_DIM)
            kw = kbuf[pl.ds(start, _NA_PAIR_KEYS), hs]
            vw = vbuf[pl.ds(start, _NA_PAIR_KEYS), hs]
            bias = jnp.concatenate([bias_ref[r - rs0, 0, h], bias_ref[r + 1 - rs1, rs1 - rs0, h]], axis=0)
            s = _dot_nt(q_ref[qs, hs], kw) * ATTN_SCALE + bias
            m = jnp.max(s, axis=-1, keepdims=True)
            p = jnp.exp(s - m)
            denom = jnp.sum(p, axis=-1, keepdims=True)
            o_ref[qs, hs] = (_dot(p.astype(BF16), vw) / denom).astype(o_ref.dtype)


def _neighbourhood_attention(q, k, v, bias, batch, seq_len):
    n = q.shape[0]
    grid_rows = seq_len // GRID_W
    assert grid_rows >= 2 * NA_ROWS and seq_len % _NA_CHUNK == 0
    chunks = seq_len // _NA_CHUNK
    halos = seq_len // _NA_HALO

    def cur(b, c):
        return (b * chunks + c, 0)

    def prev(b, c):
        return (b * halos + jnp.maximum(2 * c - 1, 0), 0)

    def nxt(b, c):
        return (b * halos + jnp.minimum(2 * c + 2, halos - 1), 0)

    def kv_specs():
        return [pl.BlockSpec((_NA_HALO, B_W), prev), pl.BlockSpec((_NA_CHUNK, B_W), cur),
                pl.BlockSpec((_NA_HALO, B_W), nxt)]

    buf = pltpu.VMEM((_NA_CHUNK + 2 * _NA_HALO, B_W), BF16)
    return pl.pallas_call(
        functools.partial(_na_kernel, grid_rows=grid_rows),
        grid=(batch, chunks),
        in_specs=[_resident((NA_ROWS, 2, B_HEADS, GRID_W, _NA_PAIR_KEYS)),
                  pl.BlockSpec((_NA_CHUNK, B_W), cur), *kv_specs(), *kv_specs()],
        out_specs=pl.BlockSpec((_NA_CHUNK, B_W), cur),
        out_shape=jax.ShapeDtypeStruct((n, B_W), BF16),
        scratch_shapes=[buf, buf],
        compiler_params=_params("parallel", "parallel"),
        name="neighbourhood_attn",
    )(bias, q, k, k, k, v, v, v)


def _mem_kv_kernel(mem_ref, w_ref, k_ref, v_ref):
    kv = _dot(mem_ref[...].astype(BF16), w_ref[...])
    k_ref[...] = kv[:, :M_W].astype(k_ref.dtype)
    v_ref[...] = kv[:, M_W:].astype(v_ref.dtype)


def _memory_kv(mem2, w_mem_kv, tm):
    n = mem2.shape[0]
    out = jax.ShapeDtypeStruct((n, M_W), BF16)
    return pl.pallas_call(
        _mem_kv_kernel,
        grid=(n // tm,),
        in_specs=[pl.BlockSpec((tm, D_MODEL), lambda i: (i, 0)), _resident((D_MODEL, 2 * M_W))],
        out_specs=[pl.BlockSpec((tm, M_W), lambda i: (i, 0))] * 2,
        out_shape=[out, out],
        compiler_params=_params("parallel"),
        name="memory_kv",
    )(mem2, w_mem_kv)


def _mem_attn_kernel(q_ref, k_ref, v_ref, o_ref):
    for h in range(M_HEADS):
        hs = slice(h * HEAD_DIM, (h + 1) * HEAD_DIM)
        s = _dot_nt(q_ref[:, hs], k_ref[:, hs]) * ATTN_SCALE
        m = jnp.max(s, axis=-1, keepdims=True)
        p = jnp.exp(s - m)
        denom = jnp.sum(p, axis=-1, keepdims=True)
        o_ref[:, hs] = (_dot(p.astype(BF16), v_ref[:, hs]) / denom).astype(o_ref.dtype)


def _memory_attention(q, km, vm, batch, seq_len, tm=TOKEN_BLOCK):
    n = q.shape[0]
    n_mem = km.shape[0] // batch
    chunks = seq_len // tm
    return pl.pallas_call(
        _mem_attn_kernel,
        grid=(batch, chunks),
        in_specs=[pl.BlockSpec((tm, M_W), lambda b, c: (b * chunks + c, 0)),
                  pl.BlockSpec((n_mem, M_W), lambda b, c: (b, 0)),
                  pl.BlockSpec((n_mem, M_W), lambda b, c: (b, 0))],
        out_specs=pl.BlockSpec((tm, M_W), lambda b, c: (b * chunks + c, 0)),
        out_shape=jax.ShapeDtypeStruct((n, M_W), BF16),
        compiler_params=_params("parallel", "parallel"),
        name="memory_attn",
    )(q, km, vm)


_MERGE_COLS = 512


def _merge_kernel(x_ref, oa_ref, ob_ref, om_ref, wg0_ref, wg1_ref, wg2_ref, bg0_ref, bg1_ref,
                  bg2_ref, wa_ref, wb_ref, wm_ref, wo_ref, g_ref, b_ref, y_ref, yb_ref,
                  xb_ref, acc_ref, *, alpha):
    j = pl.program_id(1)

    @pl.when(j == 0)
    def _():
        xb_ref[...] = x_ref[...].astype(BF16)
        acc_ref[...] = jnp.zeros_like(acc_ref)

    xb = xb_ref[...]
    merged = (jax.nn.sigmoid(_dot(xb, wg0_ref[...]) + bg0_ref[...]) * _dot(oa_ref[...], wa_ref[...])
              + jax.nn.sigmoid(_dot(xb, wg1_ref[...]) + bg1_ref[...]) * _dot(ob_ref[...], wb_ref[...])
              + jax.nn.sigmoid(_dot(xb, wg2_ref[...]) + bg2_ref[...]) * _dot(om_ref[...], wm_ref[...]))
    acc_ref[...] += _dot(merged.astype(BF16), wo_ref[...])

    @pl.when(j == pl.num_programs(1) - 1)
    def _():
        y = _layer_norm(alpha * x_ref[...] + acc_ref[...], g_ref[...], b_ref[...])
        y_ref[...] = y
        yb_ref[...] = y.astype(BF16)


def _merge(x2, o_a, o_b, o_m, w_gate, b_gate, w_pa, w_pb, w_pm, w_out, ln_g, ln_b, alpha,
           tm=TOKEN_BLOCK):
    n = x2.shape[0]
    steps = D_MODEL // _MERGE_COLS
    row = lambda i, j: (i, 0)

    def gate_w(branch):
        return pl.BlockSpec((D_MODEL, _MERGE_COLS), lambda i, j: (0, branch * steps + j))

    def gate_b(branch):
        return pl.BlockSpec((1, _MERGE_COLS), lambda i, j: (0, branch * steps + j))

    col = lambda i, j: (0, j)
    vec = pl.BlockSpec((1, D_MODEL), lambda i, j: (0, 0))

    def once(shape):
        return pl.BlockSpec(shape, row, pipeline_mode=pl.Buffered(1))

    return pl.pallas_call(
        functools.partial(_merge_kernel, alpha=alpha),
        grid=(n // tm, steps),
        in_specs=[pl.BlockSpec((tm, D_MODEL), row), once((tm, A_Q_W)), once((tm, B_W)),
                  once((tm, M_W)),
                  gate_w(0), gate_w(1), gate_w(2), gate_b(0), gate_b(1), gate_b(2),
                  pl.BlockSpec((A_Q_W, _MERGE_COLS), col), pl.BlockSpec((B_W, _MERGE_COLS), col),
                  pl.BlockSpec((M_W, _MERGE_COLS), col),
                  pl.BlockSpec((_MERGE_COLS, D_MODEL), lambda i, j: (j, 0)), vec, vec],
        out_specs=[pl.BlockSpec((tm, D_MODEL), row), pl.BlockSpec((tm, D_MODEL), row)],
        out_shape=[jax.ShapeDtypeStruct((n, D_MODEL), F32), jax.ShapeDtypeStruct((n, D_MODEL), BF16)],
        scratch_shapes=[pltpu.VMEM((tm, D_MODEL), BF16), pltpu.VMEM((tm, D_MODEL), F32)],
        compiler_params=_params("parallel", "arbitrary"),
        name="merge_out_ln",
    )(x2, o_a, o_b, o_m, w_gate, w_gate, w_gate, b_gate, b_gate, b_gate, w_pa, w_pb, w_pm, w_out,
      ln_g, ln_b)


def _take_top(s, count, row_id):
    order = jnp.full(s.shape, float(count), F32)
    vals, ids = [], []
    for a in range(count):
        m = jnp.max(s, axis=0, keepdims=True)
        rid = jnp.min(jnp.where(s == m, row_id, _NO_ROW), axis=0, keepdims=True)
        hit = row_id == rid
        order = jnp.where(hit, float(a), order)
        s = jnp.where(hit, -jnp.inf, s)
        vals.append(m)
        ids.append(rid)
    return vals, ids, order


def _sorting_network(n):
    pairs = []
    p = 1
    while p < n:
        k = p
        while k >= 1:
            for j in range(k % p, n - k, 2 * k):
                for i in range(min(k, n - j - k)):
                    if (i + j) // (2 * p) == (i + j + k) // (2 * p):
                        pairs.append((i + j, i + j + k))
            k //= 2
        p *= 2
    return pairs


def _top_sorted(s):
    assert s.shape[0] == PEER_TOPK * SUBLANES
    v = [s[g * SUBLANES:(g + 1) * SUBLANES] for g in range(PEER_TOPK)]

    def exchange(i, j):
        v[i], v[j] = jnp.maximum(v[i], v[j]), jnp.minimum(v[i], v[j])

    for i, j in _sorting_network(PEER_TOPK):
        exchange(i, j)
    shift = SUBLANES // 2
    while shift >= 1:
        other = [pltpu.roll(x, shift, 0) for x in v]
        v = [jnp.maximum(v[i], other[PEER_TOPK - 1 - i]) for i in range(PEER_TOPK)]
        d = PEER_TOPK // 2
        while d >= 1:
            for i in range(PEER_TOPK):
                if not i & d:
                    exchange(i, i + d)
            d //= 2
        shift //= 2
    return v


def _count_greater(t, x):
    assert len(t) == 16
    b3 = t[7] > x
    b2 = jnp.where(b3, t[11], t[3]) > x
    b1 = jnp.where(b3, jnp.where(b2, t[13], t[9]), jnp.where(b2, t[5], t[1])) > x
    upper = jnp.where(b2, jnp.where(b1, t[14], t[12]), jnp.where(b1, t[10], t[8]))
    lower = jnp.where(b2, jnp.where(b1, t[6], t[4]), jnp.where(b1, t[2], t[0]))
    b0 = jnp.where(b3, upper, lower) > x
    count = (jnp.where(b3, 8.0, 0.0) + jnp.where(b2, 4.0, 0.0) + jnp.where(b1, 2.0, 0.0)
             + jnp.where(b0, 1.0, 0.0))
    return jnp.where(t[15] > x, 16.0, count)


def _top_untied(s):
    t = _top_sorted(s)
    groups = [s[g * SUBLANES:(g + 1) * SUBLANES] for g in range(s.shape[0] // SUBLANES)]
    order = jnp.concatenate([_count_greater(t, x) for x in groups], axis=0)
    tied = jnp.zeros_like(t[0])
    for a in range(PEER_TOPK - 1):
        tied = jnp.maximum(tied, jnp.where(t[a] == t[a + 1], 1.0, 0.0))
    reach = jnp.zeros_like(t[0])
    for x in groups:
        reach = reach + jnp.where(x >= t[PEER_TOPK - 1], 1.0, 0.0)
    shift = SUBLANES // 2
    while shift >= 1:
        reach = reach + pltpu.roll(reach, shift, 0)
        shift //= 2
    tied = jnp.maximum(tied, jnp.where(reach != float(PEER_TOPK), 1.0, 0.0))
    return [x[0:1] for x in t], order, tied


_NO_ROW = 1e9
_CAND_COUNTS = tuple(PEER_TOPK // (a + 1) for a in range(PEER_TOPK))
_CAND_ROWS = -(-sum(_CAND_COUNTS) // SUBLANES) * SUBLANES


_PREP_LANES = 2 * LANES


def _candidate_ids():
    ids = [a * PEER_TOPK + b for a, nb in enumerate(_CAND_COUNTS) for b in range(nb)]
    ids += [_NO_ROW] * (_CAND_ROWS - len(ids))
    return jnp.broadcast_to(jnp.asarray(ids, F32)[:, None], (_CAND_ROWS, _PREP_LANES))


def _peer_prep_kernel(xb_ref, wq_ref, k1_ref, k2_ref, cid_ref, n1_ref, e1_ref, r2_ref, e2_ref,
                      st_ref):
    tm = xb_ref.shape[0]
    q = _dot(xb_ref[...], wq_ref[...]).astype(BF16)
    for h in range(PEER_HEADS):
        q1 = q[:, h * PEER_QDIM:h * PEER_QDIM + PEER_HALF]
        q2 = q[:, h * PEER_QDIM + PEER_HALF:(h + 1) * PEER_QDIM]
        st_ref[2 * h] = _dot_nt(k1_ref[...], q1)
        st_ref[2 * h + 1] = _dot_nt(k2_ref[...], q2)
    width = _PREP_LANES
    lane_chunks = tm // width

    def body(it, carry):
        h = it // lane_chunks
        lanes = pl.ds(pl.multiple_of((it % lane_chunks) * width, width), width)
        s1 = st_ref[2 * h, :, lanes]
        s2 = st_ref[2 * h + 1, :, lanes]
        t1, order1, tied1 = _top_untied(s1)
        t2, order2, tied2 = _top_untied(s2)

        def with_ties():
            key_id = lax.broadcasted_iota(jnp.int32, (N_KEYS, width), 0).astype(F32)
            v1, _, o1 = _take_top(s1, PEER_TOPK, key_id)
            v2, _, o2 = _take_top(s2, PEER_TOPK, key_id)
            return tuple(v1), o1, tuple(v2), o2

        t1, order1, t2, order2 = lax.cond(jnp.max(jnp.maximum(tied1, tied2)) > 0.0, with_ties,
                                          lambda: (tuple(t1), order1, tuple(t2), order2))
        t2_all = jnp.concatenate(t2, axis=0)
        pad = jnp.full((_CAND_ROWS - sum(_CAND_COUNTS), width), -jnp.inf, F32)
        cand = jnp.concatenate(
            [t1[a] + t2_all[:nb] for a, nb in enumerate(_CAND_COUNTS)] + [pad], axis=0)
        top, cid, _ = _take_top(cand, PEER_TOPK, cid_ref[...])
        a_id = lax.broadcasted_iota(jnp.int32, (PEER_TOPK, width), 0).astype(F32)
        taken = jnp.zeros((PEER_TOPK, width), F32)
        z = jnp.zeros((1, width), F32)
        for k in range(PEER_TOPK):
            taken = taken + jnp.where(a_id == jnp.floor(cid[k] * (1.0 / PEER_TOPK)), 1.0, 0.0)
            z = z + jnp.exp(top[k] - top[0])
        n1 = jnp.zeros((N_KEYS, width), F32)
        for a in range(PEER_TOPK):
            n1 = jnp.where(order1 == float(a), taken[a:a + 1, :], n1)
        keys = pl.ds(pl.multiple_of(h * N_KEYS, N_KEYS), N_KEYS)
        n1_ref[keys, lanes] = n1
        e1_ref[keys, lanes] = jnp.exp(s1 - t1[0])
        r2_ref[keys, lanes] = order2.astype(r2_ref.dtype)
        e2_ref[keys, lanes] = (jnp.exp(s2 - t2[0]) / z).astype(e2_ref.dtype)
        return carry

    lax.fori_loop(0, PEER_HEADS * lane_chunks, body, 0)


def _peer_prep(x1b, w_q, keys1, keys2, tm=TOKEN_BLOCK):
    n = x1b.shape[0]
    by_key1 = jax.ShapeDtypeStruct((PEER_HEADS * N_KEYS, n), F32)
    by_key2 = jax.ShapeDtypeStruct((PEER_HEADS * N_KEYS, n), BF16)
    spec = pl.BlockSpec((PEER_HEADS * N_KEYS, tm), lambda i: (0, i))
    return pl.pallas_call(
        _peer_prep_kernel,
        grid=(n // tm,),
        in_specs=[pl.BlockSpec((tm, D_MODEL), lambda i: (i, 0)),
                  _resident((D_MODEL, PEER_HEADS * PEER_QDIM)),
                  _resident((N_KEYS, PEER_HALF)), _resident((N_KEYS, PEER_HALF)),
                  _resident((_CAND_ROWS, _PREP_LANES))],
        out_specs=[spec] * 4,
        out_shape=[by_key1, by_key1, by_key2, by_key2],
        scratch_shapes=[pltpu.VMEM((2 * PEER_HEADS, N_KEYS, tm), F32)],
        compiler_params=_params("parallel"),
        name="peer_retrieve",
    )(x1b, w_q, keys1, keys2, _candidate_ids())


def _peer_kernel(xb_ref, u_ref, vt_ref, n1_ref, e1_ref, r2_ref, e2_ref, x_ref, g_ref, b_ref,
                 y_ref, acc_ref, a_ref, w_ref, *, alpha):
    j = pl.program_id(1)
    te, tm = a_ref.shape

    @pl.when(j == 0)
    def _():
        acc_ref[...] = jnp.zeros_like(acc_ref)

    a_ref[...] = _dot_nt(u_ref[...], xb_ref[...])
    keys_per_step = te // N_KEYS
    first_key = pl.multiple_of(j * keys_per_step, SUBLANES)

    def first_key_row(ref, h, sub, ls):
        return ref[pl.ds(h * N_KEYS + first_key, SUBLANES), ls][sub:sub + 1]

    chunk = 2 * LANES

    def packed_rows(row):
        tile_rows = 2 * SUBLANES
        tile = jnp.broadcast_to(row, (tile_rows, chunk)).astype(BF16)
        return jnp.concatenate([tile] * (N_KEYS // tile_rows), axis=0)

    for sub in range(keys_per_step):
        es = slice(sub * N_KEYS, (sub + 1) * N_KEYS)
        for tc in range(tm // chunk):
            ls = slice(tc * chunk, (tc + 1) * chunk)
            gate = jnp.zeros((N_KEYS, chunk), BF16)
            for h in range(PEER_HEADS):
                ks = slice(h * N_KEYS, (h + 1) * N_KEYS)
                n1 = packed_rows(first_key_row(n1_ref, h, sub, ls))
                e1 = packed_rows(first_key_row(e1_ref, h, sub, ls))
                gate = gate + jnp.where(r2_ref[ks, ls] < n1, e2_ref[ks, ls] * e1,
                                        jnp.zeros_like(gate))
            a = a_ref[es, ls]
            gelu = 0.5 * a * (1.0 + lax.erf(a * (2.0 ** -0.5)))
            w_ref[es, ls] = gelu.astype(BF16) * gate

    acc_ref[...] += _dot(vt_ref[...], w_ref[...])

    @pl.when(j == pl.num_programs(1) - 1)
    def _():
        ff = acc_ref[...].T
        y_ref[...] = _layer_norm(alpha * x_ref[...] + ff, g_ref[...], b_ref[...])


def _peer(x1, x1b, u, vt, n1, e1, r2, e2, ln_g, ln_b, alpha, tm=TOKEN_BLOCK, te=PEER_TILE):
    n = x1.shape[0]
    n_experts = u.shape[0]
    assert (te // N_KEYS) % SUBLANES == 0
    row = lambda i, j: (i, 0)
    sel = pl.BlockSpec((PEER_HEADS * N_KEYS, tm), lambda i, j: (0, i),
                       pipeline_mode=pl.Buffered(1))
    vec = pl.BlockSpec((1, D_MODEL), lambda i, j: (0, 0))
    return pl.pallas_call(
        functools.partial(_peer_kernel, alpha=alpha),
        grid=(n // tm, n_experts // te),
        in_specs=[pl.BlockSpec((tm, D_MODEL), row),
                  pl.BlockSpec((te, D_MODEL), lambda i, j: (j, 0)),
                  pl.BlockSpec((D_MODEL, te), lambda i, j: (0, j)),
                  sel, sel, sel, sel,
                  pl.BlockSpec((tm, D_MODEL), row, pipeline_mode=pl.Buffered(1)), vec, vec],
        out_specs=pl.BlockSpec((tm, D_MODEL), row),
        out_shape=jax.ShapeDtypeStruct((n, D_MODEL), F32),
        scratch_shapes=[pltpu.VMEM((D_MODEL, tm), F32), pltpu.VMEM((te, tm), F32),
                        pltpu.VMEM((te, tm), BF16)],
        compiler_params=_params("parallel", "arbitrary"),
        name="peer_experts_ln",
    )(x1b, u, vt, n1, e1, r2, e2, x1, ln_g, ln_b)


def _encoder_layer(x, mem, p, alpha):
    batch, seq_len, _ = x.shape
    x2 = x.reshape(batch * seq_len, D_MODEL)
    q_a, k_a, v_a, q_b, k_b, v_b, q_m = _project(x2, p["w_qkv"], _rope_tables(seq_len), seq_len)
    o_a = _window_attention(q_a, k_a, v_a, p["a_sink"], batch, seq_len)
    o_b = _neighbourhood_attention(q_b, k_b, v_b, p["na_bias"], batch, seq_len)
    km, vm = _memory_kv(mem.reshape(-1, D_MODEL), p["w_mem_kv"], tm=mem.shape[1])
    o_m = _memory_attention(q_m, km, vm, batch, seq_len)
    x1, x1b = _merge(x2, o_a, o_b, o_m, p["w_gate"], p["b_gate"], p["w_proj_a"], p["w_proj_b"],
                     p["w_proj_m"], p["w_out"], p["ln1_g"], p["ln1_b"], alpha)
    n1, e1, r2, e2 = _peer_prep(x1b, p["w_peer_q"], p["peer_keys1"], p["peer_keys2"])
    y = _peer(x1, x1b, p["peer_u"], p["peer_vt"], n1, e1, r2, e2, p["ln2_g"], p["ln2_b"], alpha)
    return y.reshape(batch, seq_len, D_MODEL)


def _layer_params(w_in, b_gate, a_sink, na_rpb, w_mem_kv, w_proj_a, w_proj_b, w_proj_m, w_out,
                  ln1_g, ln1_b, w_peer_q, peer_keys1, peer_keys2, peer_u, peer_v, ln2_g, ln2_b):
    row = lambda v: v.reshape(1, -1).astype(F32)
    return {
        "w_qkv": w_in[:, :QKV_W].astype(BF16),
        "w_gate": w_in[:, QKV_W:].astype(BF16),
        "b_gate": row(b_gate),
        "a_sink": a_sink.astype(F32),
        "na_bias": _na_bias_table(na_rpb),
        "w_mem_kv": w_mem_kv.astype(BF16),
        "w_proj_a": w_proj_a.astype(BF16),
        "w_proj_b": w_proj_b.astype(BF16),
        "w_proj_m": w_proj_m.astype(BF16),
        "w_out": w_out.astype(BF16),
        "ln1_g": row(ln1_g),
        "ln1_b": row(ln1_b),
        "w_peer_q": w_peer_q.astype(BF16),
        "peer_keys1": peer_keys1.astype(BF16),
        "peer_keys2": peer_keys2.astype(BF16),
        "peer_u": peer_u.astype(BF16),
        "peer_vt": peer_v.T.astype(BF16),
        "ln2_g": row(ln2_g),
        "ln2_b": row(ln2_b),
    }


def kernel(x_prompt, x_sample, mem_prompt, mem_sample, w_in, b_gate, a_sink, na_rpb, w_mem_kv,
           w_proj_a, w_proj_b, w_proj_m, w_out, ln1_g, ln1_b, w_peer_q, peer_keys1, peer_keys2,
           peer_u, peer_v, ln2_g, ln2_b):
    depth = w_in.shape[0]
    alpha = (2.0 * depth) ** 0.25
    y_prompt, y_sample = x_prompt, x_sample
    for l in range(depth):
        p = _layer_params(w_in[l], b_gate[l], a_sink[l], na_rpb[l], w_mem_kv[l], w_proj_a[l],
                          w_proj_b[l], w_proj_m[l], w_out[l], ln1_g[l], ln1_b[l], w_peer_q[l],
                          peer_keys1[l], peer_keys2[l], peer_u[l], peer_v[l], ln2_g[l], ln2_b[l])
        y_prompt = _encoder_layer(y_prompt, mem_prompt, p, alpha)
        y_sample = _encoder_layer(y_sample, mem_sample, p, alpha)
    return (y_prompt, y_sample)
```

```python
import functools

import jax
import jax.numpy as jnp
from jax import lax
from jax.experimental import pallas as pl
from jax.experimental.pallas import tpu as pltpu

D_MODEL = 2048
HEAD_DIM = 128
A_HEADS = 8
A_KV_HEADS = 2
A_GROUP = A_HEADS // A_KV_HEADS
WINDOW = 128
ROPE_THETA = 500000.0
ROPE_DIM = HEAD_DIM // 4
B_HEADS = 4
GRID_W = 64
NA_ROWS = 8
NA_COLS = 16
M_HEADS = 4
A_Q_W = A_HEADS * HEAD_DIM
A_KV_W = A_KV_HEADS * HEAD_DIM
B_W = B_HEADS * HEAD_DIM
M_W = M_HEADS * HEAD_DIM
QKV_W = A_Q_W + 2 * A_KV_W + 3 * B_W + M_W
PEER_HEADS = 8
PEER_QDIM = 256
PEER_HALF = PEER_QDIM // 2
N_KEYS = 128
PEER_TOPK = 16
LN_EPS = 1e-5
NEG_INF = -1e30
ATTN_SCALE = HEAD_DIM ** -0.5

LANES = 128
SUBLANES = 8
VMEM_LIMIT = 56 * 1024 * 1024
TOKEN_BLOCK = 512
PEER_TILE = 1024

BF16 = jnp.bfloat16
F32 = jnp.float32

_PROJ_PIECES = (("q_a", A_Q_W, True), ("k_a", A_KV_W, True), ("v_a", A_KV_W, False),
                ("q_b", B_W, False), ("k_b", B_W, False), ("v_b", B_W, False), ("q_m", M_W, False))
_COL_CHUNK = 512


def _dot(a, b):
    return jnp.dot(a, b, preferred_element_type=F32)


def _dot_nt(a, b):
    return lax.dot_general(a, b, (((1,), (1,)), ((), ())), preferred_element_type=F32)


def _params(*semantics):
    return pltpu.CompilerParams(dimension_semantics=semantics, vmem_limit_bytes=VMEM_LIMIT)


def _resident(shape):
    return pl.BlockSpec(shape, lambda *_: (0,) * len(shape), pipeline_mode=pl.Buffered(1))


def _layer_norm(h, g, b):
    mu = jnp.mean(h, axis=-1, keepdims=True)
    hc = h - mu
    var = jnp.mean(hc * hc, axis=-1, keepdims=True)
    return hc * lax.rsqrt(var + LN_EPS) * g + b


def _proj_kernel(x_ref, w_ref, cos_ref, sin_up_ref, sin_dn_ref, *out_refs):
    xb = x_ref[...].astype(BF16)
    off = 0
    for (_, width, rotary), o_ref in zip(_PROJ_PIECES, out_refs):
        for c0 in range(0, width, _COL_CHUNK):
            cw = min(_COL_CHUNK, width - c0)
            acc = _dot(xb, w_ref[:, off + c0:off + c0 + cw])
            if rotary:
                reps = cw // HEAD_DIM
                cos = jnp.concatenate([cos_ref[...]] * reps, axis=1)
                sin_up = jnp.concatenate([sin_up_ref[...]] * reps, axis=1)
                sin_dn = jnp.concatenate([sin_dn_ref[...]] * reps, axis=1)
                half = ROPE_DIM // 2
                acc = (acc * cos + pltpu.roll(acc, half, 1) * sin_up
                       + pltpu.roll(acc, cw - half, 1) * sin_dn)
            o_ref[:, c0:c0 + cw] = acc.astype(o_ref.dtype)
        off += width


def _rope_tables(seq_len):
    half = ROPE_DIM // 2
    inv_freq = ROPE_THETA ** (-jnp.arange(half, dtype=F32) * 2.0 / ROPE_DIM)
    ang = jnp.arange(seq_len, dtype=F32)[:, None] * inv_freq[None, :]
    cos, sin = jnp.cos(ang), jnp.sin(ang)
    zeros = jnp.zeros((seq_len, HEAD_DIM - ROPE_DIM), F32)
    zh = jnp.zeros((seq_len, half), F32)
    cos_t = jnp.concatenate([cos, cos, zeros + 1.0], axis=1)
    sin_up = jnp.concatenate([zh, sin, zeros], axis=1)
    sin_dn = jnp.concatenate([-sin, zh, zeros], axis=1)
    return cos_t, sin_up, sin_dn


def _project(x2, w_qkv, rope, seq_len, tm=TOKEN_BLOCK):
    n = x2.shape[0]
    blocks_per_seq = seq_len // tm
    tab_spec = pl.BlockSpec((tm, HEAD_DIM), lambda i: (i % blocks_per_seq, 0))
    return pl.pallas_call(
        _proj_kernel,
        grid=(n // tm,),
        in_specs=[pl.BlockSpec((tm, D_MODEL), lambda i: (i, 0)),
                  _resident((D_MODEL, QKV_W)), tab_spec, tab_spec, tab_spec],
        out_specs=[pl.BlockSpec((tm, w), lambda i: (i, 0)) for _, w, _ in _PROJ_PIECES],
        out_shape=[jax.ShapeDtypeStruct((n, w), BF16) for _, w, _ in _PROJ_PIECES],
        compiler_params=_params("parallel"),
        name="qkv_proj",
    )(x2, w_qkv, *rope)


_WIN_CHUNK = 2048
_WIN_SUB = _WIN_CHUNK // WINDOW


def _win_attn_kernel(sink_ref, q_ref, kp_ref, kc_ref, kn_ref, vp_ref, vc_ref, vn_ref, o_ref):
    c = pl.program_id(1)
    rows = A_GROUP * WINDOW
    qi = lax.broadcasted_iota(jnp.int32, (rows, WINDOW), 0) % WINDOW
    kj = lax.broadcasted_iota(jnp.int32, (rows, WINDOW), 1)
    below = kj >= qi
    above = kj <= qi
    prev_pad = jnp.where(c > 0, 0.0, NEG_INF)
    next_pad = jnp.where(c < pl.num_programs(1) - 1, 0.0, NEG_INF)
    for g in range(A_KV_HEADS):
        hs = slice(g * HEAD_DIM, (g + 1) * HEAD_DIM)
        sink = jnp.concatenate(
            [jnp.full((WINDOW, 1), sink_ref[A_GROUP * g + u], F32) for u in range(A_GROUP)], axis=0)
        for sb in range(_WIN_SUB):
            rs = slice(sb * WINDOW, (sb + 1) * WINDOW)
            q = jnp.concatenate(
                [q_ref[rs, (A_GROUP * g + u) * HEAD_DIM:(A_GROUP * g + u + 1) * HEAD_DIM]
                 for u in range(A_GROUP)], axis=0)
            if sb == 0:
                k_prev, v_prev, pad_prev = kp_ref[:, hs], vp_ref[:, hs], prev_pad
            else:
                ps = slice((sb - 1) * WINDOW, sb * WINDOW)
                k_prev, v_prev, pad_prev = kc_ref[ps, hs], vc_ref[ps, hs], 0.0
            if sb == _WIN_SUB - 1:
                k_next, v_next, pad_next = kn_ref[:, hs], vn_ref[:, hs], next_pad
            else:
                ns = slice((sb + 1) * WINDOW, (sb + 2) * WINDOW)
                k_next, v_next, pad_next = kc_ref[ns, hs], vc_ref[ns, hs], 0.0
            s_prev = jnp.where(below, _dot_nt(q, k_prev) * ATTN_SCALE, NEG_INF) + pad_prev
            s_cur = _dot_nt(q, kc_ref[rs, hs]) * ATTN_SCALE
            s_next = jnp.where(above, _dot_nt(q, k_next) * ATTN_SCALE, NEG_INF) + pad_next
            m = jnp.max(jnp.maximum(jnp.maximum(s_prev, s_cur), s_next), axis=-1, keepdims=True)
            m = jnp.maximum(m, sink)
            p_prev, p_cur, p_next = jnp.exp(s_prev - m), jnp.exp(s_cur - m), jnp.exp(s_next - m)
            denom = jnp.sum(p_prev + p_cur + p_next, axis=-1, keepdims=True) + jnp.exp(sink - m)
            o = (_dot(p_prev.astype(BF16), v_prev) + _dot(p_cur.astype(BF16), vc_ref[rs, hs])
                 + _dot(p_next.astype(BF16), v_next)) / denom
            for u in range(A_GROUP):
                h = A_GROUP * g + u
                o_ref[rs, h * HEAD_DIM:(h + 1) * HEAD_DIM] = (
                    o[u * WINDOW:(u + 1) * WINDOW].astype(o_ref.dtype))


def _window_attention(q, k, v, sink, batch, seq_len):
    n = q.shape[0]
    chunks = seq_len // _WIN_CHUNK
    blocks = seq_len // WINDOW

    def cur(b, c):
        return (b * chunks + c, 0)

    def prev(b, c):
        return (b * blocks + jnp.maximum(c * _WIN_SUB - 1, 0), 0)

    def nxt(b, c):
        return (b * blocks + jnp.minimum((c + 1) * _WIN_SUB, blocks - 1), 0)

    def kv_specs():
        return [pl.BlockSpec((WINDOW, A_KV_W), prev), pl.BlockSpec((_WIN_CHUNK, A_KV_W), cur),
                pl.BlockSpec((WINDOW, A_KV_W), nxt)]

    return pl.pallas_call(
        _win_attn_kernel,
        grid=(batch, chunks),
        in_specs=[pl.BlockSpec(memory_space=pltpu.SMEM), pl.BlockSpec((_WIN_CHUNK, A_Q_W), cur),
                  *kv_specs(), *kv_specs()],
        out_specs=pl.BlockSpec((_WIN_CHUNK, A_Q_W), cur),
        out_shape=jax.ShapeDtypeStruct((n, A_Q_W), BF16),
        compiler_params=_params("parallel", "parallel"),
        name="window_attn",
    )(sink, q, k, k, k, v, v, v)


_NA_KEYS = NA_ROWS * GRID_W
_NA_CHUNK = NA_ROWS * GRID_W
_NA_HALO = _NA_CHUNK // 2
_NA_PAIR_KEYS = (NA_ROWS + 2) * GRID_W


def _na_bias_table(rpb):
    c = jnp.arange(GRID_W)
    cs = jnp.clip(c - NA_COLS // 2, 0, GRID_W - NA_COLS)
    col_ok = (c[None, :] >= cs[:, None]) & (c[None, :] < cs[:, None] + NA_COLS)
    dc = jnp.clip(c[None, :] - c[:, None], -(NA_COLS - 1), NA_COLS - 1) + NA_COLS - 1
    onehot = dc[:, :, None] == jnp.arange(2 * NA_COLS - 1)[None, None, :]
    by_col = jnp.sum(jnp.where(onehot[None, None], rpb.astype(F32)[:, :, None, None, :], 0.0), axis=-1)
    by_col = jnp.where(col_ok[None, None], by_col, NEG_INF)
    pats = [by_col[:, NA_ROWS - 1 - p:2 * NA_ROWS - 1 - p] for p in range(NA_ROWS)]
    bias = jnp.stack(pats, axis=0)
    bias = bias.transpose(0, 1, 3, 2, 4).reshape(NA_ROWS, B_HEADS, GRID_W, _NA_KEYS)
    spare = _NA_PAIR_KEYS - _NA_KEYS
    shifted = [jnp.pad(bias, ((0, 0), (0, 0), (0, 0), (s * GRID_W, spare - s * GRID_W)),
                       constant_values=NEG_INF) for s in range(2)]
    return jnp.stack(shifted, axis=1)


def _na_kernel(bias_ref, q_ref, kp_ref, kc_ref, kn_ref, vp_ref, vc_ref, vn_ref, o_ref,
               kbuf, vbuf, *, grid_rows):
    c = pl.program_id(1)
    kbuf[0:_NA_HALO] = kp_ref[...]
    kbuf[_NA_HALO:_NA_HALO + _NA_CHUNK] = kc_ref[...]
    kbuf[_NA_HALO + _NA_CHUNK:] = kn_ref[...]
    vbuf[0:_NA_HALO] = vp_ref[...]
    vbuf[_NA_HALO:_NA_HALO + _NA_CHUNK] = vc_ref[...]
    vbuf[_NA_HALO + _NA_CHUNK:] = vn_ref[...]
    halo_rows = _NA_HALO // GRID_W

    def window_start(r):
        return jnp.clip(r - NA_ROWS // 2, 0, grid_rows - NA_ROWS)

    for i in range(0, NA_ROWS, 2):
        r = c * NA_ROWS + i
        rs0, rs1 = window_start(r), window_start(r + 1)
        start = pl.multiple_of((rs0 - (c * NA_ROWS - halo_rows)) * GRID_W, GRID_W)
        qs = slice(i * GRID_W, (i + 2) * GRID_W)
        for h in range(B_HEADS):
            hs = slice(h * HEAD_DIM, (h + 1) * HEAD_DIM)
            kw = kbuf[pl.ds(start, _NA_PAIR_KEYS), hs]
            vw = vbuf[pl.ds(start, _NA_PAIR_KEYS), hs]
            bias = jnp.concatenate([bias_ref[r - rs0, 0, h], bias_ref[r + 1 - rs1, rs1 - rs0, h]], axis=0)
            s = _dot_nt(q_ref[qs, hs], kw) * ATTN_SCALE + bias
            m = jnp.max(s, axis=-1, keepdims=True)
            p = jnp.exp(s - m)
            denom = jnp.sum(p, axis=-1, keepdims=True)
            o_ref[qs, hs] = (_dot(p.astype(BF16), vw) / denom).astype(o_ref.dtype)


def _neighbourhood_attention(q, k, v, bias, batch, seq_len):
    n = q.shape[0]
    grid_rows = seq_len // GRID_W
    assert grid_rows >= 2 * NA_ROWS and seq_len % _NA_CHUNK == 0
    chunks = seq_len // _NA_CHUNK
    halos = seq_len // _NA_HALO

    def cur(b, c):
        return (b * chunks + c, 0)

    def prev(b, c):
        return (b * halos + jnp.maximum(2 * c - 1, 0), 0)

    def nxt(b, c):
        return (b * halos + jnp.minimum(2 * c + 2, halos - 1), 0)

    def kv_specs():
        return [pl.BlockSpec((_NA_HALO, B_W), prev), pl.BlockSpec((_NA_CHUNK, B_W), cur),
                pl.BlockSpec((_NA_HALO, B_W), nxt)]

    buf = pltpu.VMEM((_NA_CHUNK + 2 * _NA_HALO, B_W), BF16)
    return pl.pallas_call(
        functools.partial(_na_kernel, grid_rows=grid_rows),
        grid=(batch, chunks),
        in_specs=[_resident((NA_ROWS, 2, B_HEADS, GRID_W, _NA_PAIR_KEYS)),
                  pl.BlockSpec((_NA_CHUNK, B_W), cur), *kv_specs(), *kv_specs()],
        out_specs=pl.BlockSpec((_NA_CHUNK, B_W), cur),
        out_shape=jax.ShapeDtypeStruct((n, B_W), BF16),
        scratch_shapes=[buf, buf],
        compiler_params=_params("parallel", "parallel"),
        name="neighbourhood_attn",
    )(bias, q, k, k, k, v, v, v)


def _mem_kv_kernel(mem_ref, w_ref, k_ref, v_ref):
    kv = _dot(mem_ref[...].astype(BF16), w_ref[...])
    k_ref[...] = kv[:, :M_W].astype(k_ref.dtype)
    v_ref[...] = kv[:, M_W:].astype(v_ref.dtype)


def _memory_kv(mem2, w_mem_kv, tm):
    n = mem2.shape[0]
    out = jax.ShapeDtypeStruct((n, M_W), BF16)
    return pl.pallas_call(
        _mem_kv_kernel,
        grid=(n // tm,),
        in_specs=[pl.BlockSpec((tm, D_MODEL), lambda i: (i, 0)), _resident((D_MODEL, 2 * M_W))],
        out_specs=[pl.BlockSpec((tm, M_W), lambda i: (i, 0))] * 2,
        out_shape=[out, out],
        compiler_params=_params("parallel"),
        name="memory_kv",
    )(mem2, w_mem_kv)


def _mem_attn_kernel(q_ref, k_ref, v_ref, o_ref):
    for h in range(M_HEADS):
        hs = slice(h * HEAD_DIM, (h + 1) * HEAD_DIM)
        s = _dot_nt(q_ref[:, hs], k_ref[:, hs]) * ATTN_SCALE
        m = jnp.max(s, axis=-1, keepdims=True)
        p = jnp.exp(s - m)
        denom = jnp.sum(p, axis=-1, keepdims=True)
        o_ref[:, hs] = (_dot(p.astype(BF16), v_ref[:, hs]) / denom).astype(o_ref.dtype)


def _memory_attention(q, km, vm, batch, seq_len, tm=TOKEN_BLOCK):
    n = q.shape[0]
    n_mem = km.shape[0] // batch
    chunks = seq_len // tm
    return pl.pallas_call(
        _mem_attn_kernel,
        grid=(batch, chunks),
        in_specs=[pl.BlockSpec((tm, M_W), lambda b, c: (b * chunks + c, 0)),
                  pl.BlockSpec((n_mem, M_W), lambda b, c: (b, 0)),
                  pl.BlockSpec((n_mem, M_W), lambda b, c: (b, 0))],
        out_specs=pl.BlockSpec((tm, M_W), lambda b, c: (b * chunks + c, 0)),
        out_shape=jax.ShapeDtypeStruct((n, M_W), BF16),
        compiler_params=_params("parallel", "parallel"),
        name="memory_attn",
    )(q, km, vm)


_MERGE_COLS = 512


def _merge_kernel(x_ref, oa_ref, ob_ref, om_ref, wg0_ref, wg1_ref, wg2_ref, bg0_ref, bg1_ref,
                  bg2_ref, wa_ref, wb_ref, wm_ref, wo_ref, g_ref, b_ref, y_ref, yb_ref,
                  xb_ref, acc_ref, *, alpha):
    j = pl.program_id(1)

    @pl.when(j == 0)
    def _():
        xb_ref[...] = x_ref[...].astype(BF16)
        acc_ref[...] = jnp.zeros_like(acc_ref)

    xb = xb_ref[...]
    merged = (jax.nn.sigmoid(_dot(xb, wg0_ref[...]) + bg0_ref[...]) * _dot(oa_ref[...], wa_ref[...])
              + jax.nn.sigmoid(_dot(xb, wg1_ref[...]) + bg1_ref[...]) * _dot(ob_ref[...], wb_ref[...])
              + jax.nn.sigmoid(_dot(xb, wg2_ref[...]) + bg2_ref[...]) * _dot(om_ref[...], wm_ref[...]))
    acc_ref[...] += _dot(merged.astype(BF16), wo_ref[...])

    @pl.when(j == pl.num_programs(1) - 1)
    def _():
        y = _layer_norm(alpha * x_ref[...] + acc_ref[...], g_ref[...], b_ref[...])
        y_ref[...] = y
        yb_ref[...] = y.astype(BF16)


def _merge(x2, o_a, o_b, o_m, w_gate, b_gate, w_pa, w_pb, w_pm, w_out, ln_g, ln_b, alpha,
           tm=TOKEN_BLOCK):
    n = x2.shape[0]
    steps = D_MODEL // _MERGE_COLS
    row = lambda i, j: (i, 0)

    def gate_w(branch):
        return pl.BlockSpec((D_MODEL, _MERGE_COLS), lambda i, j: (0, branch * steps + j))

    def gate_b(branch):
        return pl.BlockSpec((1, _MERGE_COLS), lambda i, j: (0, branch * steps + j))

    col = lambda i, j: (0, j)
    vec = pl.BlockSpec((1, D_MODEL), lambda i, j: (0, 0))

    def once(shape):
        return pl.BlockSpec(shape, row, pipeline_mode=pl.Buffered(1))

    return pl.pallas_call(
        functools.partial(_merge_kernel, alpha=alpha),
        grid=(n // tm, steps),
        in_specs=[pl.BlockSpec((tm, D_MODEL), row), once((tm, A_Q_W)), once((tm, B_W)),
                  once((tm, M_W)),
                  gate_w(0), gate_w(1), gate_w(2), gate_b(0), gate_b(1), gate_b(2),
                  pl.BlockSpec((A_Q_W, _MERGE_COLS), col), pl.BlockSpec((B_W, _MERGE_COLS), col),
                  pl.BlockSpec((M_W, _MERGE_COLS), col),
                  pl.BlockSpec((_MERGE_COLS, D_MODEL), lambda i, j: (j, 0)), vec, vec],
        out_specs=[pl.BlockSpec((tm, D_MODEL), row), pl.BlockSpec((tm, D_MODEL), row)],
        out_shape=[jax.ShapeDtypeStruct((n, D_MODEL), F32), jax.ShapeDtypeStruct((n, D_MODEL), BF16)],
        scratch_shapes=[pltpu.VMEM((tm, D_MODEL), BF16), pltpu.VMEM((tm, D_MODEL), F32)],
        compiler_params=_params("parallel", "arbitrary"),
        name="merge_out_ln",
    )(x2, o_a, o_b, o_m, w_gate, w_gate, w_gate, b_gate, b_gate, b_gate, w_pa, w_pb, w_pm, w_out,
      ln_g, ln_b)


def _take_top(s, count, row_id):
    order = jnp.full(s.shape, float(count), F32)
    vals, ids = [], []
    for a in range(count):
        m = jnp.max(s, axis=0, keepdims=True)
        rid = jnp.min(jnp.where(s == m, row_id, _NO_ROW), axis=0, keepdims=True)
        hit = row_id == rid
        order = jnp.where(hit, float(a), order)
        s = jnp.where(hit, -jnp.inf, s)
        vals.append(m)
        ids.append(rid)
    return vals, ids, order


def _sorting_network(n):
    pairs = []
    p = 1
    while p < n:
        k = p
        while k >= 1:
            for j in range(k % p, n - k, 2 * k):
                for i in range(min(k, n - j - k)):
                    if (i + j) // (2 * p) == (i + j + k) // (2 * p):
                        pairs.append((i + j, i + j + k))
            k //= 2
        p *= 2
    return pairs


def _top_sorted(s):
    assert s.shape[0] == PEER_TOPK * SUBLANES
    v = [s[g * SUBLANES:(g + 1) * SUBLANES] for g in range(PEER_TOPK)]

    def exchange(i, j):
        v[i], v[j] = jnp.maximum(v[i], v[j]), jnp.minimum(v[i], v[j])

    for i, j in _sorting_network(PEER_TOPK):
        exchange(i, j)
    shift = SUBLANES // 2
    while shift >= 1:
        other = [pltpu.roll(x, shift, 0) for x in v]
        v = [jnp.maximum(v[i], other[PEER_TOPK - 1 - i]) for i in range(PEER_TOPK)]
        d = PEER_TOPK // 2
        while d >= 1:
            for i in range(PEER_TOPK):
                if not i & d:
                    exchange(i, i + d)
            d //= 2
        shift //= 2
    return v


def _count_greater(t, x):
    assert len(t) == 16
    b3 = t[7] > x
    b2 = jnp.where(b3, t[11], t[3]) > x
    b1 = jnp.where(b3, jnp.where(b2, t[13], t[9]), jnp.where(b2, t[5], t[1])) > x
    upper = jnp.where(b2, jnp.where(b1, t[14], t[12]), jnp.where(b1, t[10], t[8]))
    lower = jnp.where(b2, jnp.where(b1, t[6], t[4]), jnp.where(b1, t[2], t[0]))
    b0 = jnp.where(b3, upper, lower) > x
    count = (jnp.where(b3, 8.0, 0.0) + jnp.where(b2, 4.0, 0.0) + jnp.where(b1, 2.0, 0.0)
             + jnp.where(b0, 1.0, 0.0))
    return jnp.where(t[15] > x, 16.0, count)


def _top_untied(s):
    t = _top_sorted(s)
    groups = [s[g * SUBLANES:(g + 1) * SUBLANES] for g in range(s.shape[0] // SUBLANES)]
    order = jnp.concatenate([_count_greater(t, x) for x in groups], axis=0)
    tied = jnp.zeros_like(t[0])
    for a in range(PEER_TOPK - 1):
        tied = jnp.maximum(tied, jnp.where(t[a] == t[a + 1], 1.0, 0.0))
    reach = jnp.zeros_like(t[0])
    for x in groups:
        reach = reach + jnp.where(x >= t[PEER_TOPK - 1], 1.0, 0.0)
    shift = SUBLANES // 2
    while shift >= 1:
        reach = reach + pltpu.roll(reach, shift, 0)
        shift //= 2
    tied = jnp.maximum(tied, jnp.where(reach != float(PEER_TOPK), 1.0, 0.0))
    return [x[0:1] for x in t], order, tied


_NO_ROW = 1e9
_CAND_COUNTS = tuple(PEER_TOPK // (a + 1) for a in range(PEER_TOPK))
_CAND_ROWS = -(-sum(_CAND_COUNTS) // SUBLANES) * SUBLANES


_PREP_LANES = 2 * LANES


def _candidate_ids():
    ids = [a * PEER_TOPK + b for a, nb in enumerate(_CAND_COUNTS) for b in range(nb)]
    ids += [_NO_ROW] * (_CAND_ROWS - len(ids))
    return jnp.broadcast_to(jnp.asarray(ids, F32)[:, None], (_CAND_ROWS, _PREP_LANES))


def _peer_prep_kernel(xb_ref, wq_ref, k1_ref, k2_ref, cid_ref, n1_ref, e1_ref, r2_ref, e2_ref,
                      st_ref):
    tm = xb_ref.shape[0]
    q = _dot(xb_ref[...], wq_ref[...]).astype(BF16)
    for h in range(PEER_HEADS):
        q1 = q[:, h * PEER_QDIM:h * PEER_QDIM + PEER_HALF]
        q2 = q[:, h * PEER_QDIM + PEER_HALF:(h + 1) * PEER_QDIM]
        st_ref[2 * h] = _dot_nt(k1_ref[...], q1)
        st_ref[2 * h + 1] = _dot_nt(k2_ref[...], q2)
    width = _PREP_LANES
    lane_chunks = tm // width

    def body(it, carry):
        h = it // lane_chunks
        lanes = pl.ds(pl.multiple_of((it % lane_chunks) * width, width), width)
        s1 = st_ref[2 * h, :, lanes]
        s2 = st_ref[2 * h + 1, :, lanes]
        t1, order1, tied1 = _top_untied(s1)
        t2, order2, tied2 = _top_untied(s2)

        def with_ties():
            key_id = lax.broadcasted_iota(jnp.int32, (N_KEYS, width), 0).astype(F32)
            v1, _, o1 = _take_top(s1, PEER_TOPK, key_id)
            v2, _, o2 = _take_top(s2, PEER_TOPK, key_id)
            return tuple(v1), o1, tuple(v2), o2

        t1, order1, t2, order2 = lax.cond(jnp.max(jnp.maximum(tied1, tied2)) > 0.0, with_ties,
                                          lambda: (tuple(t1), order1, tuple(t2), order2))
        t2_all = jnp.concatenate(t2, axis=0)
        pad = jnp.full((_CAND_ROWS - sum(_CAND_COUNTS), width), -jnp.inf, F32)
        cand = jnp.concatenate(
            [t1[a] + t2_all[:nb] for a, nb in enumerate(_CAND_COUNTS)] + [pad], axis=0)
        top, cid, _ = _take_top(cand, PEER_TOPK, cid_ref[...])
        a_id = lax.broadcasted_iota(jnp.int32, (PEER_TOPK, width), 0).astype(F32)
        taken = jnp.zeros((PEER_TOPK, width), F32)
        z = jnp.zeros((1, width), F32)
        for k in range(PEER_TOPK):
            taken = taken + jnp.where(a_id == jnp.floor(cid[k] * (1.0 / PEER_TOPK)), 1.0, 0.0)
            z = z + jnp.exp(top[k] - top[0])
        n1 = jnp.zeros((N_KEYS, width), F32)
        for a in range(PEER_TOPK):
            n1 = jnp.where(order1 == float(a), taken[a:a + 1, :], n1)
        keys = pl.ds(pl.multiple_of(h * N_KEYS, N_KEYS), N_KEYS)
        n1_ref[keys, lanes] = n1
        e1_ref[keys, lanes] = jnp.exp(s1 - t1[0])
        r2_ref[keys, lanes] = order2.astype(r2_ref.dtype)
        e2_ref[keys, lanes] = (jnp.exp(s2 - t2[0]) / z).astype(e2_ref.dtype)
        return carry

    lax.fori_loop(0, PEER_HEADS * lane_chunks, body, 0)


def _peer_prep(x1b, w_q, keys1, keys2, tm=TOKEN_BLOCK):
    n = x1b.shape[0]
    by_key1 = jax.ShapeDtypeStruct((PEER_HEADS * N_KEYS, n), F32)
    by_key2 = jax.ShapeDtypeStruct((PEER_HEADS * N_KEYS, n), BF16)
    spec = pl.BlockSpec((PEER_HEADS * N_KEYS, tm), lambda i: (0, i))
    return pl.pallas_call(
        _peer_prep_kernel,
        grid=(n // tm,),
        in_specs=[pl.BlockSpec((tm, D_MODEL), lambda i: (i, 0)),
                  _resident((D_MODEL, PEER_HEADS * PEER_QDIM)),
                  _resident((N_KEYS, PEER_HALF)), _resident((N_KEYS, PEER_HALF)),
                  _resident((_CAND_ROWS, _PREP_LANES))],
        out_specs=[spec] * 4,
        out_shape=[by_key1, by_key1, by_key2, by_key2],
        scratch_shapes=[pltpu.VMEM((2 * PEER_HEADS, N_KEYS, tm), F32)],
        compiler_params=_params("parallel"),
        name="peer_retrieve",
    )(x1b, w_q, keys1, keys2, _candidate_ids())


def _peer_kernel(xb_ref, u_ref, vt_ref, n1_ref, e1_ref, r2_ref, e2_ref, x_ref, g_ref, b_ref,
                 y_ref, acc_ref, a_ref, w_ref, *, alpha):
    j = pl.program_id(1)
    te, tm = a_ref.shape

    @pl.when(j == 0)
    def _():
        acc_ref[...] = jnp.zeros_like(acc_ref)

    a_ref[...] = _dot_nt(u_ref[...], xb_ref[...])
    keys_per_step = te // N_KEYS
    first_key = pl.multiple_of(j * keys_per_step, SUBLANES)

    def first_key_row(ref, h, sub, ls):
        return ref[pl.ds(h * N_KEYS + first_key, SUBLANES), ls][sub:sub + 1]

    chunk = 2 * LANES

    def packed_rows(row):
        tile_rows = 2 * SUBLANES
        tile = jnp.broadcast_to(row, (tile_rows, chunk)).astype(BF16)
        return jnp.concatenate([tile] * (N_KEYS // tile_rows), axis=0)

    for sub in range(keys_per_step):
        es = slice(sub * N_KEYS, (sub + 1) * N_KEYS)
        for tc in range(tm // chunk):
            ls = slice(tc * chunk, (tc + 1) * chunk)
            gate = jnp.zeros((N_KEYS, chunk), BF16)
            for h in range(PEER_HEADS):
                ks = slice(h * N_KEYS, (h + 1) * N_KEYS)
                n1 = packed_rows(first_key_row(n1_ref, h, sub, ls))
                e1 = packed_rows(first_key_row(e1_ref, h, sub, ls))
                gate = gate + jnp.where(r2_ref[ks, ls] < n1, e2_ref[ks, ls] * e1,
                                        jnp.zeros_like(gate))
            a = a_ref[es, ls]
            gelu = 0.5 * a * (1.0 + lax.erf(a * (2.0 ** -0.5)))
            w_ref[es, ls] = gelu.astype(BF16) * gate

    acc_ref[...] += _dot(vt_ref[...], w_ref[...])

    @pl.when(j == pl.num_programs(1) - 1)
    def _():
        ff = acc_ref[...].T
        y_ref[...] = _layer_norm(alpha * x_ref[...] + ff, g_ref[...], b_ref[...])


def _peer(x1, x1b, u, vt, n1, e1, r2, e2, ln_g, ln_b, alpha, tm=TOKEN_BLOCK, te=PEER_TILE):
    n = x1.shape[0]
    n_experts = u.shape[0]
    assert (te // N_KEYS) % SUBLANES == 0
    row = lambda i, j: (i, 0)
    sel = pl.BlockSpec((PEER_HEADS * N_KEYS, tm), lambda i, j: (0, i),
                       pipeline_mode=pl.Buffered(1))
    vec = pl.BlockSpec((1, D_MODEL), lambda i, j: (0, 0))
    return pl.pallas_call(
        functools.partial(_peer_kernel, alpha=alpha),
        grid=(n // tm, n_experts // te),
        in_specs=[pl.BlockSpec((tm, D_MODEL), row),
                  pl.BlockSpec((te, D_MODEL), lambda i, j: (j, 0)),
                  pl.BlockSpec((D_MODEL, te), lambda i, j: (0, j)),
                  sel, sel, sel, sel,
                  pl.BlockSpec((tm, D_MODEL), row), vec, vec],
        out_specs=pl.BlockSpec((tm, D_MODEL), row),
        out_shape=jax.ShapeDtypeStruct((n, D_MODEL), F32),
        scratch_shapes=[pltpu.VMEM((D_MODEL, tm), F32), pltpu.VMEM((te, tm), F32),
                        pltpu.VMEM((te, tm), BF16)],
        compiler_params=_params("parallel", "arbitrary"),
        name="peer_experts_ln",
    )(x1b, u, vt, n1, e1, r2, e2, x1, ln_g, ln_b)


def _encoder_layer(x, mem, p, alpha):
    batch, seq_len, _ = x.shape
    x2 = x.reshape(batch * seq_len, D_MODEL)
    q_a, k_a, v_a, q_b, k_b, v_b, q_m = _project(x2, p["w_qkv"], _rope_tables(seq_len), seq_len)
    o_a = _window_attention(q_a, k_a, v_a, p["a_sink"], batch, seq_len)
    o_b = _neighbourhood_attention(q_b, k_b, v_b, p["na_bias"], batch, seq_len)
    km, vm = _memory_kv(mem.reshape(-1, D_MODEL), p["w_mem_kv"], tm=mem.shape[1])
    o_m = _memory_attention(q_m, km, vm, batch, seq_len)
    x1, x1b = _merge(x2, o_a, o_b, o_m, p["w_gate"], p["b_gate"], p["w_proj_a"], p["w_proj_b"],
                     p["w_proj_m"], p["w_out"], p["ln1_g"], p["ln1_b"], alpha)
    n1, e1, r2, e2 = _peer_prep(x1b, p["w_peer_q"], p["peer_keys1"], p["peer_keys2"])
    y = _peer(x1, x1b, p["peer_u"], p["peer_vt"], n1, e1, r2, e2, p["ln2_g"], p["ln2_b"], alpha)
    return y.reshape(batch, seq_len, D_MODEL)


def _layer_params(w_in, b_gate, a_sink, na_rpb, w_mem_kv, w_proj_a, w_proj_b, w_proj_m, w_out,
                  ln1_g, ln1_b, w_peer_q, peer_keys1, peer_keys2, peer_u, peer_v, ln2_g, ln2_b):
    row = lambda v: v.reshape(1, -1).astype(F32)
    return {
        "w_qkv": w_in[:, :QKV_W].astype(BF16),
        "w_gate": w_in[:, QKV_W:].astype(BF16),
        "b_gate": row(b_gate),
        "a_sink": a_sink.astype(F32),
        "na_bias": _na_bias_table(na_rpb),
        "w_mem_kv": w_mem_kv.astype(BF16),
        "w_proj_a": w_proj_a.astype(BF16),
        "w_proj_b": w_proj_b.astype(BF16),
        "w_proj_m": w_proj_m.astype(BF16),
        "w_out": w_out.astype(BF16),
        "ln1_g": row(ln1_g),
        "ln1_b": row(ln1_b),
        "w_peer_q": w_peer_q.astype(BF16),
        "peer_keys1": peer_keys1.astype(BF16),
        "peer_keys2": peer_keys2.astype(BF16),
        "peer_u": peer_u.astype(BF16),
        "peer_vt": peer_v.T.astype(BF16),
        "ln2_g": row(ln2_g),
        "ln2_b": row(ln2_b),
    }


def kernel(x_prompt, x_sample, mem_prompt, mem_sample, w_in, b_gate, a_sink, na_rpb, w_mem_kv,
           w_proj_a, w_proj_b, w_proj_m, w_out, ln1_g, ln1_b, w_peer_q, peer_keys1, peer_keys2,
           peer_u, peer_v, ln2_g, ln2_b):
    depth = w_in.shape[0]
    alpha = (2.0 * depth) ** 0.25
    y_prompt, y_sample = x_prompt, x_sample
    for l in range(depth):
        p = _layer_params(w_in[l], b_gate[l], a_sink[l], na_rpb[l], w_mem_kv[l], w_proj_a[l],
                          w_proj_b[l], w_proj_m[l], w_out[l], ln1_g[l], ln1_b[l], w_peer_q[l],
                          peer_keys1[l], peer_keys2[l], peer_u[l], peer_v[l], ln2_g[l], ln2_b[l])
        y_prompt = _encoder_layer(y_prompt, mem_prompt, p, alpha)
        y_sample = _encoder_layer(y_sample, mem_sample, p, alpha)
    return (y_prompt, y_sample)
```

```python
import functools

import jax
import jax.numpy as jnp
from jax import lax
from jax.experimental import pallas as pl
from jax.experimental.pallas import tpu as pltpu

D_MODEL = 2048
HEAD_DIM = 128
A_HEADS = 8
A_KV_HEADS = 2
A_GROUP = A_HEADS // A_KV_HEADS
WINDOW = 128
ROPE_THETA = 500000.0
ROPE_DIM = HEAD_DIM // 4
B_HEADS = 4
GRID_W = 64
NA_ROWS = 8
NA_COLS = 16
M_HEADS = 4
A_Q_W = A_HEADS * HEAD_DIM
A_KV_W = A_KV_HEADS * HEAD_DIM
B_W = B_HEADS * HEAD_DIM
M_W = M_HEADS * HEAD_DIM
QKV_W = A_Q_W + 2 * A_KV_W + 3 * B_W + M_W
PEER_HEADS = 8
PEER_QDIM = 256
PEER_HALF = PEER_QDIM // 2
N_KEYS = 128
PEER_TOPK = 16
LN_EPS = 1e-5
NEG_INF = -1e30
ATTN_SCALE = HEAD_DIM ** -0.5

LANES = 128
SUBLANES = 8
VMEM_LIMIT = 56 * 1024 * 1024
TOKEN_BLOCK = 512
PEER_TILE = 1024

BF16 = jnp.bfloat16
F32 = jnp.float32

_PROJ_PIECES = (("q_a", A_Q_W, True), ("k_a", A_KV_W, True), ("v_a", A_KV_W, False),
                ("q_b", B_W, False), ("k_b", B_W, False), ("v_b", B_W, False), ("q_m", M_W, False))
_COL_CHUNK = 512


def _dot(a, b):
    return jnp.dot(a, b, preferred_element_type=F32)


def _dot_nt(a, b):
    return lax.dot_general(a, b, (((1,), (1,)), ((), ())), preferred_element_type=F32)


def _params(*semantics):
    return pltpu.CompilerParams(dimension_semantics=semantics, vmem_limit_bytes=VMEM_LIMIT)


def _resident(shape):
    return pl.BlockSpec(shape, lambda *_: (0,) * len(shape), pipeline_mode=pl.Buffered(1))


def _layer_norm(h, g, b):
    mu = jnp.mean(h, axis=-1, keepdims=True)
    hc = h - mu
    var = jnp.mean(hc * hc, axis=-1, keepdims=True)
    return hc * lax.rsqrt(var + LN_EPS) * g + b


def _proj_kernel(x_ref, w_ref, cos_ref, sin_up_ref, sin_dn_ref, *out_refs):
    xb = x_ref[...].astype(BF16)
    off = 0
    for (_, width, rotary), o_ref in zip(_PROJ_PIECES, out_refs):
        for c0 in range(0, width, _COL_CHUNK):
            cw = min(_COL_CHUNK, width - c0)
            acc = _dot(xb, w_ref[:, off + c0:off + c0 + cw])
            if rotary:
                reps = cw // HEAD_DIM
                cos = jnp.concatenate([cos_ref[...]] * reps, axis=1)
                sin_up = jnp.concatenate([sin_up_ref[...]] * reps, axis=1)
                sin_dn = jnp.concatenate([sin_dn_ref[...]] * reps, axis=1)
                half = ROPE_DIM // 2
                acc = (acc * cos + pltpu.roll(acc, half, 1) * sin_up
                       + pltpu.roll(acc, cw - half, 1) * sin_dn)
            o_ref[:, c0:c0 + cw] = acc.astype(o_ref.dtype)
        off += width


def _rope_tables(seq_len):
    half = ROPE_DIM // 2
    inv_freq = ROPE_THETA ** (-jnp.arange(half, dtype=F32) * 2.0 / ROPE_DIM)
    ang = jnp.arange(seq_len, dtype=F32)[:, None] * inv_freq[None, :]
    cos, sin = jnp.cos(ang), jnp.sin(ang)
    zeros = jnp.zeros((seq_len, HEAD_DIM - ROPE_DIM), F32)
    zh = jnp.zeros((seq_len, half), F32)
    cos_t = jnp.concatenate([cos, cos, zeros + 1.0], axis=1)
    sin_up = jnp.concatenate([zh, sin, zeros], axis=1)
    sin_dn = jnp.concatenate([-sin, zh, zeros], axis=1)
    return cos_t, sin_up, sin_dn


def _project(x2, w_qkv, rope, seq_len, tm=TOKEN_BLOCK):
    n = x2.shape[0]
    blocks_per_seq = seq_len // tm
    tab_spec = pl.BlockSpec((tm, HEAD_DIM), lambda i: (i % blocks_per_seq, 0))
    return pl.pallas_call(
        _proj_kernel,
        grid=(n // tm,),
        in_specs=[pl.BlockSpec((tm, D_MODEL), lambda i: (i, 0)),
                  _resident((D_MODEL, QKV_W)), tab_spec, tab_spec, tab_spec],
        out_specs=[pl.BlockSpec((tm, w), lambda i: (i, 0)) for _, w, _ in _PROJ_PIECES],
        out_shape=[jax.ShapeDtypeStruct((n, w), BF16) for _, w, _ in _PROJ_PIECES],
        compiler_params=_params("parallel"),
        name="qkv_proj",
    )(x2, w_qkv, *rope)


_WIN_CHUNK = 2048
_WIN_SUB = _WIN_CHUNK // WINDOW


def _win_attn_kernel(sink_ref, q_ref, kp_ref, kc_ref, kn_ref, vp_ref, vc_ref, vn_ref, o_ref):
    c = pl.program_id(1)
    rows = A_GROUP * WINDOW
    qi = lax.broadcasted_iota(jnp.int32, (rows, WINDOW), 0) % WINDOW
    kj = lax.broadcasted_iota(jnp.int32, (rows, WINDOW), 1)
    below = kj >= qi
    above = kj <= qi
    prev_pad = jnp.where(c > 0, 0.0, NEG_INF)
    next_pad = jnp.where(c < pl.num_programs(1) - 1, 0.0, NEG_INF)
    for g in range(A_KV_HEADS):
        hs = slice(g * HEAD_DIM, (g + 1) * HEAD_DIM)
        sink = jnp.concatenate(
            [jnp.full((WINDOW, 1), sink_ref[A_GROUP * g + u], F32) for u in range(A_GROUP)], axis=0)
        for sb in range(_WIN_SUB):
            rs = slice(sb * WINDOW, (sb + 1) * WINDOW)
            q = jnp.concatenate(
                [q_ref[rs, (A_GROUP * g + u) * HEAD_DIM:(A_GROUP * g + u + 1) * HEAD_DIM]
                 for u in range(A_GROUP)], axis=0)
            if sb == 0:
                k_prev, v_prev, pad_prev = kp_ref[:, hs], vp_ref[:, hs], prev_pad
            else:
                ps = slice((sb - 1) * WINDOW, sb * WINDOW)
                k_prev, v_prev, pad_prev = kc_ref[ps, hs], vc_ref[ps, hs], 0.0
            if sb == _WIN_SUB - 1:
                k_next, v_next, pad_next = kn_ref[:, hs], vn_ref[:, hs], next_pad
            else:
                ns = slice((sb + 1) * WINDOW, (sb + 2) * WINDOW)
                k_next, v_next, pad_next = kc_ref[ns, hs], vc_ref[ns, hs], 0.0
            s_prev = jnp.where(below, _dot_nt(q, k_prev) * ATTN_SCALE, NEG_INF) + pad_prev
            s_cur = _dot_nt(q, kc_ref[rs, hs]) * ATTN_SCALE
            s_next = jnp.where(above, _dot_nt(q, k_next) * ATTN_SCALE, NEG_INF) + pad_next
            m = jnp.max(jnp.maximum(jnp.maximum(s_prev, s_cur), s_next), axis=-1, keepdims=True)
            m = jnp.maximum(m, sink)
            p_prev, p_cur, p_next = jnp.exp(s_prev - m), jnp.exp(s_cur - m), jnp.exp(s_next - m)
            denom = jnp.sum(p_prev + p_cur + p_next, axis=-1, keepdims=True) + jnp.exp(sink - m)
            o = (_dot(p_prev.astype(BF16), v_prev) + _dot(p_cur.astype(BF16), vc_ref[rs, hs])
                 + _dot(p_next.astype(BF16), v_next)) / denom
            for u in range(A_GROUP):
                h = A_GROUP * g + u
                o_ref[rs, h * HEAD_DIM:(h + 1) * HEAD_DIM] = (
                    o[u * WINDOW:(u + 1) * WINDOW].astype(o_ref.dtype))


def _window_attention(q, k, v, sink, batch, seq_len):
    n = q.shape[0]
    chunks = seq_len // _WIN_CHUNK
    blocks = seq_len // WINDOW

    def cur(b, c):
        return (b * chunks + c, 0)

    def prev(b, c):
        return (b * blocks + jnp.maximum(c * _WIN_SUB - 1, 0), 0)

    def nxt(b, c):
        return (b * blocks + jnp.minimum((c + 1) * _WIN_SUB, blocks - 1), 0)

    def kv_specs():
        return [pl.BlockSpec((WINDOW, A_KV_W), prev), pl.BlockSpec((_WIN_CHUNK, A_KV_W), cur),
                pl.BlockSpec((WINDOW, A_KV_W), nxt)]

    return pl.pallas_call(
        _win_attn_kernel,
        grid=(batch, chunks),
        in_specs=[pl.BlockSpec(memory_space=pltpu.SMEM), pl.BlockSpec((_WIN_CHUNK, A_Q_W), cur),
                  *kv_specs(), *kv_specs()],
        out_specs=pl.BlockSpec((_WIN_CHUNK, A_Q_W), cur),
        out_shape=jax.ShapeDtypeStruct((n, A_Q_W), BF16),
        compiler_params=_params("parallel", "parallel"),
        name="window_attn",
    )(sink, q, k, k, k, v, v, v)


_NA_KEYS = NA_ROWS * GRID_W
_NA_CHUNK = NA_ROWS * GRID_W
_NA_HALO = _NA_CHUNK // 2
_NA_PAIR_KEYS = (NA_ROWS + 2) * GRID_W


def _na_bias_table(rpb):
    c = jnp.arange(GRID_W)
    cs = jnp.clip(c - NA_COLS // 2, 0, GRID_W - NA_COLS)
    col_ok = (c[None, :] >= cs[:, None]) & (c[None, :] < cs[:, None] + NA_COLS)
    dc = jnp.clip(c[None, :] - c[:, None], -(NA_COLS - 1), NA_COLS - 1) + NA_COLS - 1
    onehot = dc[:, :, None] == jnp.arange(2 * NA_COLS - 1)[None, None, :]
    by_col = jnp.sum(jnp.where(onehot[None, None], rpb.astype(F32)[:, :, None, None, :], 0.0), axis=-1)
    by_col = jnp.where(col_ok[None, None], by_col, NEG_INF)
    pats = [by_col[:, NA_ROWS - 1 - p:2 * NA_ROWS - 1 - p] for p in range(NA_ROWS)]
    bias = jnp.stack(pats, axis=0)
    bias = bias.transpose(0, 1, 3, 2, 4).reshape(NA_ROWS, B_HEADS, GRID_W, _NA_KEYS)
    spare = _NA_PAIR_KEYS - _NA_KEYS
    shifted = [jnp.pad(bias, ((0, 0), (0, 0), (0, 0), (s * GRID_W, spare - s * GRID_W)),
                       constant_values=NEG_INF) for s in range(2)]
    return jnp.stack(shifted, axis=1)


def _na_kernel(bias_ref, q_ref, kp_ref, kc_ref, kn_ref, vp_ref, vc_ref, vn_ref, o_ref,
               kbuf, vbuf, *, grid_rows):
    c = pl.program_id(1)
    kbuf[0:_NA_HALO] = kp_ref[...]
    kbuf[_NA_HALO:_NA_HALO + _NA_CHUNK] = kc_ref[...]
    kbuf[_NA_HALO + _NA_CHUNK:] = kn_ref[...]
    vbuf[0:_NA_HALO] = vp_ref[...]
    vbuf[_NA_HALO:_NA_HALO + _NA_CHUNK] = vc_ref[...]
    vbuf[_NA_HALO + _NA_CHUNK:] = vn_ref[...]
    halo_rows = _NA_HALO // GRID_W

    def window_start(r):
        return jnp.clip(r - NA_ROWS // 2, 0, grid_rows - NA_ROWS)

    for i in range(0, NA_ROWS, 2):
        r = c * NA_ROWS + i
        rs0, rs1 = window_start(r), window_start(r + 1)
        start = pl.multiple_of((rs0 - (c * NA_ROWS - halo_rows)) * GRID_W, GRID_W)
        qs = slice(i * GRID_W, (i + 2) * GRID_W)
        for h in range(B_HEADS):
            hs = slice(h * HEAD_DIM, (h + 1) * HEAD_DIM)
            kw = kbuf[pl.ds(start, _NA_PAIR_KEYS), hs]
            vw = vbuf[pl.ds(start, _NA_PAIR_KEYS), hs]
            bias = jnp.concatenate([bias_ref[r - rs0, 0, h], bias_ref[r + 1 - rs1, rs1 - rs0, h]], axis=0)
            s = _dot_nt(q_ref[qs, hs], kw) * ATTN_SCALE + bias
            m = jnp.max(s, axis=-1, keepdims=True)
            p = jnp.exp(s - m)
            denom = jnp.sum(p, axis=-1, keepdims=True)
            o_ref[qs, hs] = (_dot(p.astype(BF16), vw) / denom).astype(o_ref.dtype)


def _neighbourhood_attention(q, k, v, bias, batch, seq_len):
    n = q.shape[0]
    grid_rows = seq_len // GRID_W
    assert grid_rows >= 2 * NA_ROWS and seq_len % _NA_CHUNK == 0
    chunks = seq_len // _NA_CHUNK
    halos = seq_len // _NA_HALO

    def cur(b, c):
        return (b * chunks + c, 0)

    def prev(b, c):
        return (b * halos + jnp.maximum(2 * c - 1, 0), 0)

    def nxt(b, c):
        return (b * halos + jnp.minimum(2 * c + 2, halos - 1), 0)

    def kv_specs():
        return [pl.BlockSpec((_NA_HALO, B_W), prev), pl.BlockSpec((_NA_CHUNK, B_W), cur),
                pl.BlockSpec((_NA_HALO, B_W), nxt)]

    buf = pltpu.VMEM((_NA_CHUNK + 2 * _NA_HALO, B_W), BF16)
    return pl.pallas_call(
        functools.partial(_na_kernel, grid_rows=grid_rows),
        grid=(batch, chunks),
        in_specs=[_resident((NA_ROWS, 2, B_HEADS, GRID_W, _NA_PAIR_KEYS)),
                  pl.BlockSpec((_NA_CHUNK, B_W), cur), *kv_specs(), *kv_specs()],
        out_specs=pl.BlockSpec((_NA_CHUNK, B_W), cur),
        out_shape=jax.ShapeDtypeStruct((n, B_W), BF16),
        scratch_shapes=[buf, buf],
        compiler_params=_params("parallel", "parallel"),
        name="neighbourhood_attn",
    )(bias, q, k, k, k, v, v, v)


def _mem_kv_kernel(mem_ref, w_ref, k_ref, v_ref):
    kv = _dot(mem_ref[...].astype(BF16), w_ref[...])
    k_ref[...] = kv[:, :M_W].astype(k_ref.dtype)
    v_ref[...] = kv[:, M_W:].astype(v_ref.dtype)


def _memory_kv(mem2, w_mem_kv, tm):
    n = mem2.shape[0]
    out = jax.ShapeDtypeStruct((n, M_W), BF16)
    return pl.pallas_call(
        _mem_kv_kernel,
        grid=(n // tm,),
        in_specs=[pl.BlockSpec((tm, D_MODEL), lambda i: (i, 0)), _resident((D_MODEL, 2 * M_W))],
        out_specs=[pl.BlockSpec((tm, M_W), lambda i: (i, 0))] * 2,
        out_shape=[out, out],
        compiler_params=_params("parallel"),
        name="memory_kv",
    )(mem2, w_mem_kv)


def _mem_attn_kernel(q_ref, k_ref, v_ref, o_ref):
    for h in range(M_HEADS):
        hs = slice(h * HEAD_DIM, (h + 1) * HEAD_DIM)
        s = _dot_nt(q_ref[:, hs], k_ref[:, hs]) * ATTN_SCALE
        m = jnp.max(s, axis=-1, keepdims=True)
        p = jnp.exp(s - m)
        denom = jnp.sum(p, axis=-1, keepdims=True)
        o_ref[:, hs] = (_dot(p.astype(BF16), v_ref[:, hs]) / denom).astype(o_ref.dtype)


def _memory_attention(q, km, vm, batch, seq_len, tm=TOKEN_BLOCK):
    n = q.shape[0]
    n_mem = km.shape[0] // batch
    chunks = seq_len // tm
    return pl.pallas_call(
        _mem_attn_kernel,
        grid=(batch, chunks),
        in_specs=[pl.BlockSpec((tm, M_W), lambda b, c: (b * chunks + c, 0)),
                  pl.BlockSpec((n_mem, M_W), lambda b, c: (b, 0)),
                  pl.BlockSpec((n_mem, M_W), lambda b, c: (b, 0))],
        out_specs=pl.BlockSpec((tm, M_W), lambda b, c: (b * chunks + c, 0)),
        out_shape=jax.ShapeDtypeStruct((n, M_W), BF16),
        compiler_params=_params("parallel", "parallel"),
        name="memory_attn",
    )(q, km, vm)


_MERGE_COLS = 512


def _merge_kernel(x_ref, oa_ref, ob_ref, om_ref, wg0_ref, wg1_ref, wg2_ref, bg0_ref, bg1_ref,
                  bg2_ref, wa_ref, wb_ref, wm_ref, wo_ref, g_ref, b_ref, y_ref, yb_ref,
                  xb_ref, acc_ref, *, alpha):
    j = pl.program_id(1)

    @pl.when(j == 0)
    def _():
        xb_ref[...] = x_ref[...].astype(BF16)
        acc_ref[...] = jnp.zeros_like(acc_ref)

    xb = xb_ref[...]
    merged = (jax.nn.sigmoid(_dot(xb, wg0_ref[...]) + bg0_ref[...]) * _dot(oa_ref[...], wa_ref[...])
              + jax.nn.sigmoid(_dot(xb, wg1_ref[...]) + bg1_ref[...]) * _dot(ob_ref[...], wb_ref[...])
              + jax.nn.sigmoid(_dot(xb, wg2_ref[...]) + bg2_ref[...]) * _dot(om_ref[...], wm_ref[...]))
    acc_ref[...] += _dot(merged.astype(BF16), wo_ref[...])

    @pl.when(j == pl.num_programs(1) - 1)
    def _():
        y = _layer_norm(alpha * x_ref[...] + acc_ref[...], g_ref[...], b_ref[...])
        y_ref[...] = y
        yb_ref[...] = y.astype(BF16)


def _merge(x2, o_a, o_b, o_m, w_gate, b_gate, w_pa, w_pb, w_pm, w_out, ln_g, ln_b, alpha,
           tm=TOKEN_BLOCK):
    n = x2.shape[0]
    steps = D_MODEL // _MERGE_COLS
    row = lambda i, j: (i, 0)

    def gate_w(branch):
        return pl.BlockSpec((D_MODEL, _MERGE_COLS), lambda i, j: (0, branch * steps + j))

    def gate_b(branch):
        return pl.BlockSpec((1, _MERGE_COLS), lambda i, j: (0, branch * steps + j))

    col = lambda i, j: (0, j)
    vec = pl.BlockSpec((1, D_MODEL), lambda i, j: (0, 0))

    def once(shape):
        return pl.BlockSpec(shape, row, pipeline_mode=pl.Buffered(1))

    return pl.pallas_call(
        functools.partial(_merge_kernel, alpha=alpha),
        grid=(n // tm, steps),
        in_specs=[pl.BlockSpec((tm, D_MODEL), row), once((tm, A_Q_W)), once((tm, B_W)),
                  once((tm, M_W)),
                  gate_w(0), gate_w(1), gate_w(2), gate_b(0), gate_b(1), gate_b(2),
                  pl.BlockSpec((A_Q_W, _MERGE_COLS), col), pl.BlockSpec((B_W, _MERGE_COLS), col),
                  pl.BlockSpec((M_W, _MERGE_COLS), col),
                  pl.BlockSpec((_MERGE_COLS, D_MODEL), lambda i, j: (j, 0)), vec, vec],
        out_specs=[pl.BlockSpec((tm, D_MODEL), row), pl.BlockSpec((tm, D_MODEL), row)],
        out_shape=[jax.ShapeDtypeStruct((n, D_MODEL), F32), jax.ShapeDtypeStruct((n, D_MODEL), BF16)],
        scratch_shapes=[pltpu.VMEM((tm, D_MODEL), BF16), pltpu.VMEM((tm, D_MODEL), F32)],
        compiler_params=_params("parallel", "arbitrary"),
        name="merge_out_ln",
    )(x2, o_a, o_b, o_m, w_gate, w_gate, w_gate, b_gate, b_gate, b_gate, w_pa, w_pb, w_pm, w_out,
      ln_g, ln_b)


def _take_top(s, count, row_id):
    order = jnp.full(s.shape, float(count), F32)
    vals, ids = [], []
    for a in range(count):
        m = jnp.max(s, axis=0, keepdims=True)
        rid = jnp.min(jnp.where(s == m, row_id, _NO_ROW), axis=0, keepdims=True)
        hit = row_id == rid
        order = jnp.where(hit, float(a), order)
        s = jnp.where(hit, -jnp.inf, s)
        vals.append(m)
        ids.append(rid)
    return vals, ids, order


def _sorting_network(n):
    pairs = []
    p = 1
    while p < n:
        k = p
        while k >= 1:
            for j in range(k % p, n - k, 2 * k):
                for i in range(min(k, n - j - k)):
                    if (i + j) // (2 * p) == (i + j + k) // (2 * p):
                        pairs.append((i + j, i + j + k))
            k //= 2
        p *= 2
    return pairs


def _top_sorted(s):
    assert s.shape[0] == PEER_TOPK * SUBLANES
    v = [s[g * SUBLANES:(g + 1) * SUBLANES] for g in range(PEER_TOPK)]

    def exchange(i, j):
        v[i], v[j] = jnp.maximum(v[i], v[j]), jnp.minimum(v[i], v[j])

    for i, j in _sorting_network(PEER_TOPK):
        exchange(i, j)
    shift = SUBLANES // 2
    while shift >= 1:
        other = [pltpu.roll(x, shift, 0) for x in v]
        v = [jnp.maximum(v[i], other[PEER_TOPK - 1 - i]) for i in range(PEER_TOPK)]
        d = PEER_TOPK // 2
        while d >= 1:
            for i in range(PEER_TOPK):
                if not i & d:
                    exchange(i, i + d)
            d //= 2
        shift //= 2
    return v


def _count_greater(t, x):
    assert len(t) == 16
    b3 = t[7] > x
    b2 = jnp.where(b3, t[11], t[3]) > x
    b1 = jnp.where(b3, jnp.where(b2, t[13], t[9]), jnp.where(b2, t[5], t[1])) > x
    upper = jnp.where(b2, jnp.where(b1, t[14], t[12]), jnp.where(b1, t[10], t[8]))
    lower = jnp.where(b2, jnp.where(b1, t[6], t[4]), jnp.where(b1, t[2], t[0]))
    b0 = jnp.where(b3, upper, lower) > x
    count = (jnp.where(b3, 8.0, 0.0) + jnp.where(b2, 4.0, 0.0) + jnp.where(b1, 2.0, 0.0)
             + jnp.where(b0, 1.0, 0.0))
    return jnp.where(t[15] > x, 16.0, count)


def _top_untied(s):
    t = _top_sorted(s)
    groups = [s[g * SUBLANES:(g + 1) * SUBLANES] for g in range(s.shape[0] // SUBLANES)]
    order = jnp.concatenate([_count_greater(t, x) for x in groups], axis=0)
    tied = jnp.zeros_like(t[0])
    for a in range(PEER_TOPK - 1):
        tied = jnp.maximum(tied, jnp.where(t[a] == t[a + 1], 1.0, 0.0))
    reach = jnp.zeros_like(t[0])
    for x in groups:
        reach = reach + jnp.where(x >= t[PEER_TOPK - 1], 1.0, 0.0)
    shift = SUBLANES // 2
    while shift >= 1:
        reach = reach + pltpu.roll(reach, shift, 0)
        shift //= 2
    tied = jnp.maximum(tied, jnp.where(reach != float(PEER_TOPK), 1.0, 0.0))
    return [x[0:1] for x in t], order, tied


_NO_ROW = 1e9
_CAND_COUNTS = tuple(PEER_TOPK // (a + 1) for a in range(PEER_TOPK))
_CAND_ROWS = -(-sum(_CAND_COUNTS) // SUBLANES) * SUBLANES


_PREP_LANES = 2 * LANES


def _candidate_ids():
    ids = [a * PEER_TOPK + b for a, nb in enumerate(_CAND_COUNTS) for b in range(nb)]
    ids += [_NO_ROW] * (_CAND_ROWS - len(ids))
    return jnp.broadcast_to(jnp.asarray(ids, F32)[:, None], (_CAND_ROWS, _PREP_LANES))


def _peer_prep_kernel(xb_ref, wq_ref, k1_ref, k2_ref, cid_ref, n1_ref, e1_ref, r2_ref, e2_ref,
                      st_ref):
    tm = xb_ref.shape[0]
    q = _dot(xb_ref[...], wq_ref[...]).astype(BF16)
    for h in range(PEER_HEADS):
        q1 = q[:, h * PEER_QDIM:h * PEER_QDIM + PEER_HALF]
        q2 = q[:, h * PEER_QDIM + PEER_HALF:(h + 1) * PEER_QDIM]
        st_ref[2 * h] = _dot_nt(k1_ref[...], q1)
        st_ref[2 * h + 1] = _dot_nt(k2_ref[...], q2)
    width = _PREP_LANES
    lane_chunks = tm // width

    def body(it, carry):
        h = it // lane_chunks
        lanes = pl.ds(pl.multiple_of((it % lane_chunks) * width, width), width)
        s1 = st_ref[2 * h, :, lanes]
        s2 = st_ref[2 * h + 1, :, lanes]
        t1, order1, tied1 = _top_untied(s1)
        t2, order2, tied2 = _top_untied(s2)

        def with_ties():
            key_id = lax.broadcasted_iota(jnp.int32, (N_KEYS, width), 0).astype(F32)
            v1, _, o1 = _take_top(s1, PEER_TOPK, key_id)
            v2, _, o2 = _take_top(s2, PEER_TOPK, key_id)
            return tuple(v1), o1, tuple(v2), o2

        t1, order1, t2, order2 = lax.cond(jnp.max(jnp.maximum(tied1, tied2)) > 0.0, with_ties,
                                          lambda: (tuple(t1), order1, tuple(t2), order2))
        t2_all = jnp.concatenate(t2, axis=0)
        pad = jnp.full((_CAND_ROWS - sum(_CAND_COUNTS), width), -jnp.inf, F32)
        cand = jnp.concatenate(
            [t1[a] + t2_all[:nb] for a, nb in enumerate(_CAND_COUNTS)] + [pad], axis=0)
        top, cid, _ = _take_top(cand, PEER_TOPK, cid_ref[...])
        a_id = lax.broadcasted_iota(jnp.int32, (PEER_TOPK, width), 0).astype(F32)
        taken = jnp.zeros((PEER_TOPK, width), F32)
        z = jnp.zeros((1, width), F32)
        for k in range(PEER_TOPK):
            taken = taken + jnp.where(a_id == jnp.floor(cid[k] * (1.0 / PEER_TOPK)), 1.0, 0.0)
            z = z + jnp.exp(top[k] - top[0])
        n1 = jnp.zeros((N_KEYS, width), F32)
        for a in range(PEER_TOPK):
            n1 = jnp.where(order1 == float(a), taken[a:a + 1, :], n1)
        keys = pl.ds(pl.multiple_of(h * N_KEYS, N_KEYS), N_KEYS)
        n1_ref[keys, lanes] = n1
        e1_ref[keys, lanes] = jnp.exp(s1 - t1[0])
        r2_ref[keys, lanes] = order2.astype(r2_ref.dtype)
        e2_ref[keys, lanes] = (jnp.exp(s2 - t2[0]) / z).astype(e2_ref.dtype)
        return carry

    lax.fori_loop(0, PEER_HEADS * lane_chunks, body, 0)


def _peer_prep(x1b, w_q, keys1, keys2, tm=TOKEN_BLOCK):
    n = x1b.shape[0]
    by_key1 = jax.ShapeDtypeStruct((PEER_HEADS * N_KEYS, n), F32)
    by_key2 = jax.ShapeDtypeStruct((PEER_HEADS * N_KEYS, n), BF16)
    spec = pl.BlockSpec((PEER_HEADS * N_KEYS, tm), lambda i: (0, i))
    return pl.pallas_call(
        _peer_prep_kernel,
        grid=(n // tm,),
        in_specs=[pl.BlockSpec((tm, D_MODEL), lambda i: (i, 0)),
                  _resident((D_MODEL, PEER_HEADS * PEER_QDIM)),
                  _resident((N_KEYS, PEER_HALF)), _resident((N_KEYS, PEER_HALF)),
                  _resident((_CAND_ROWS, _PREP_LANES))],
        out_specs=[spec] * 4,
        out_shape=[by_key1, by_key1, by_key2, by_key2],
        scratch_shapes=[pltpu.VMEM((2 * PEER_HEADS, N_KEYS, tm), F32)],
        compiler_params=_params("parallel"),
        name="peer_retrieve",
    )(x1b, w_q, keys1, keys2, _candidate_ids())


def _peer_kernel(xb_ref, u_ref, vt_ref, n1_ref, e1_ref, r2_ref, e2_ref, x_ref, g_ref, b_ref,
                 y_ref, acc_ref, a_ref, w_ref, *, alpha):
    j = pl.program_id(1)
    te, tm = a_ref.shape

    @pl.when(j == 0)
    def _():
        acc_ref[...] = jnp.zeros_like(acc_ref)

    a_ref[...] = _dot_nt(u_ref[...], xb_ref[...])
    keys_per_step = te // N_KEYS

    def first_key_row(ref, h, sub, ls):
        return ref[h, :, ls][sub:sub + 1]

    chunk = 2 * LANES

    def packed_rows(row):
        tile_rows = 2 * SUBLANES
        tile = jnp.broadcast_to(row, (tile_rows, chunk)).astype(BF16)
        return jnp.concatenate([tile] * (N_KEYS // tile_rows), axis=0)

    for sub in range(keys_per_step):
        es = slice(sub * N_KEYS, (sub + 1) * N_KEYS)
        for tc in range(tm // chunk):
            ls = slice(tc * chunk, (tc + 1) * chunk)
            gate = jnp.zeros((N_KEYS, chunk), BF16)
            for h in range(PEER_HEADS):
                ks = slice(h * N_KEYS, (h + 1) * N_KEYS)
                n1 = packed_rows(first_key_row(n1_ref, h, sub, ls))
                e1 = packed_rows(first_key_row(e1_ref, h, sub, ls))
                gate = gate + jnp.where(r2_ref[ks, ls] < n1, e2_ref[ks, ls] * e1,
                                        jnp.zeros_like(gate))
            a = a_ref[es, ls]
            gelu = 0.5 * a * (1.0 + lax.erf(a * (2.0 ** -0.5)))
            w_ref[es, ls] = gelu.astype(BF16) * gate

    acc_ref[...] += _dot(vt_ref[...], w_ref[...])

    @pl.when(j == pl.num_programs(1) - 1)
    def _():
        ff = acc_ref[...].T
        y_ref[...] = _layer_norm(alpha * x_ref[...] + ff, g_ref[...], b_ref[...])


def _peer(x1, x1b, u, vt, n1, e1, r2, e2, ln_g, ln_b, alpha, tm=TOKEN_BLOCK, te=PEER_TILE):
    n = x1.shape[0]
    n_experts = u.shape[0]
    assert (te // N_KEYS) % SUBLANES == 0
    row = lambda i, j: (i, 0)
    keys_per_step = te // N_KEYS
    by_key1 = pl.BlockSpec((PEER_HEADS, None, keys_per_step, tm), lambda i, j: (0, j, 0, i))
    by_key2 = pl.BlockSpec((PEER_HEADS * N_KEYS, tm), lambda i, j: (0, i))
    n1, e1 = (a.reshape(PEER_HEADS, N_KEYS // keys_per_step, keys_per_step, n) for a in (n1, e1))
    vec = pl.BlockSpec((1, D_MODEL), lambda i, j: (0, 0))
    return pl.pallas_call(
        functools.partial(_peer_kernel, alpha=alpha),
        grid=(n // tm, n_experts // te),
        in_specs=[pl.BlockSpec((tm, D_MODEL), row),
                  pl.BlockSpec((te, D_MODEL), lambda i, j: (j, 0)),
                  pl.BlockSpec((D_MODEL, te), lambda i, j: (0, j)),
                  by_key1, by_key1, by_key2, by_key2,
                  pl.BlockSpec((tm, D_MODEL), row), vec, vec],
        out_specs=pl.BlockSpec((tm, D_MODEL), row),
        out_shape=jax.ShapeDtypeStruct((n, D_MODEL), F32),
        scratch_shapes=[pltpu.VMEM((D_MODEL, tm), F32), pltpu.VMEM((te, tm), F32),
                        pltpu.VMEM((te, tm), BF16)],
        compiler_params=_params("parallel", "arbitrary"),
        name="peer_experts_ln",
    )(x1b, u, vt, n1, e1, r2, e2, x1, ln_g, ln_b)


def _encoder_layer(x, mem, p, alpha):
    batch, seq_len, _ = x.shape
    x2 = x.reshape(batch * seq_len, D_MODEL)
    q_a, k_a, v_a, q_b, k_b, v_b, q_m = _project(x2, p["w_qkv"], _rope_tables(seq_len), seq_len)
    o_a = _window_attention(q_a, k_a, v_a, p["a_sink"], batch, seq_len)
    o_b = _neighbourhood_attention(q_b, k_b, v_b, p["na_bias"], batch, seq_len)
    km, vm = _memory_kv(mem.reshape(-1, D_MODEL), p["w_mem_kv"], tm=mem.shape[1])
    o_m = _memory_attention(q_m, km, vm, batch, seq_len)
    x1, x1b = _merge(x2, o_a, o_b, o_m, p["w_gate"], p["b_gate"], p["w_proj_a"], p["w_proj_b"],
                     p["w_proj_m"], p["w_out"], p["ln1_g"], p["ln1_b"], alpha)
    n1, e1, r2, e2 = _peer_prep(x1b, p["w_peer_q"], p["peer_keys1"], p["peer_keys2"])
    y = _peer(x1, x1b, p["peer_u"], p["peer_vt"], n1, e1, r2, e2, p["ln2_g"], p["ln2_b"], alpha)
    return y.reshape(batch, seq_len, D_MODEL)


def _layer_params(w_in, b_gate, a_sink, na_rpb, w_mem_kv, w_proj_a, w_proj_b, w_proj_m, w_out,
                  ln1_g, ln1_b, w_peer_q, peer_keys1, peer_keys2, peer_u, peer_v, ln2_g, ln2_b):
    row = lambda v: v.reshape(1, -1).astype(F32)
    return {
        "w_qkv": w_in[:, :QKV_W].astype(BF16),
        "w_gate": w_in[:, QKV_W:].astype(BF16),
        "b_gate": row(b_gate),
        "a_sink": a_sink.astype(F32),
        "na_bias": _na_bias_table(na_rpb),
        "w_mem_kv": w_mem_kv.astype(BF16),
        "w_proj_a": w_proj_a.astype(BF16),
        "w_proj_b": w_proj_b.astype(BF16),
        "w_proj_m": w_proj_m.astype(BF16),
        "w_out": w_out.astype(BF16),
        "ln1_g": row(ln1_g),
        "ln1_b": row(ln1_b),
        "w_peer_q": w_peer_q.astype(BF16),
        "peer_keys1": peer_keys1.astype(BF16),
        "peer_keys2": peer_keys2.astype(BF16),
        "peer_u": peer_u.astype(BF16),
        "peer_vt": peer_v.T.astype(BF16),
        "ln2_g": row(ln2_g),
        "ln2_b": row(ln2_b),
    }


def kernel(x_prompt, x_sample, mem_prompt, mem_sample, w_in, b_gate, a_sink, na_rpb, w_mem_kv,
           w_proj_a, w_proj_b, w_proj_m, w_out, ln1_g, ln1_b, w_peer_q, peer_keys1, peer_keys2,
           peer_u, peer_v, ln2_g, ln2_b):
    depth = w_in.shape[0]
    alpha = (2.0 * depth) ** 0.25
    y_prompt, y_sample = x_prompt, x_sample
    for l in range(depth):
        p = _layer_params(w_in[l], b_gate[l], a_sink[l], na_rpb[l], w_mem_kv[l], w_proj_a[l],
                          w_proj_b[l], w_proj_m[l], w_out[l], ln1_g[l], ln1_b[l], w_peer_q[l],
                          peer_keys1[l], peer_keys2[l], peer_u[l], peer_v[l], ln2_g[l], ln2_b[l])
        y_prompt = _encoder_layer(y_prompt, mem_prompt, p, alpha)
        y_sample = _encoder_layer(y_sample, mem_sample, p, alpha)
    return (y_prompt, y_sample)
```

```python
import functools

import jax
import jax.numpy as jnp
from jax import lax
from jax.experimental import pallas as pl
from jax.experimental.pallas import tpu as pltpu

D_MODEL = 2048
HEAD_DIM = 128
A_HEADS = 8
A_KV_HEADS = 2
A_GROUP = A_HEADS // A_KV_HEADS
WINDOW = 128
ROPE_THETA = 500000.0
ROPE_DIM = HEAD_DIM // 4
B_HEADS = 4
GRID_W = 64
NA_ROWS = 8
NA_COLS = 16
M_HEADS = 4
A_Q_W = A_HEADS * HEAD_DIM
A_KV_W = A_KV_HEADS * HEAD_DIM
B_W = B_HEADS * HEAD_DIM
M_W = M_HEADS * HEAD_DIM
QKV_W = A_Q_W + 2 * A_KV_W + 3 * B_W + M_W
PEER_HEADS = 8
PEER_QDIM = 256
PEER_HALF = PEER_QDIM // 2
N_KEYS = 128
PEER_TOPK = 16
LN_EPS = 1e-5
NEG_INF = -1e30
ATTN_SCALE = HEAD_DIM ** -0.5

LANES = 128
SUBLANES = 8
VMEM_LIMIT = 56 * 1024 * 1024
TOKEN_BLOCK = 512
PEER_TILE = 1024

BF16 = jnp.bfloat16
F32 = jnp.float32

_PROJ_PIECES = (("q_a", A_Q_W, True), ("k_a", A_KV_W, True), ("v_a", A_KV_W, False),
                ("q_b", B_W, False), ("k_b", B_W, False), ("v_b", B_W, False), ("q_m", M_W, False))
_COL_CHUNK = 512


def _dot(a, b):
    return jnp.dot(a, b, preferred_element_type=F32)


def _dot_nt(a, b):
    return lax.dot_general(a, b, (((1,), (1,)), ((), ())), preferred_element_type=F32)


def _params(*semantics):
    return pltpu.CompilerParams(dimension_semantics=semantics, vmem_limit_bytes=VMEM_LIMIT)


def _resident(shape):
    return pl.BlockSpec(shape, lambda *_: (0,) * len(shape), pipeline_mode=pl.Buffered(1))


def _layer_norm(h, g, b):
    mu = jnp.mean(h, axis=-1, keepdims=True)
    hc = h - mu
    var = jnp.mean(hc * hc, axis=-1, keepdims=True)
    return hc * lax.rsqrt(var + LN_EPS) * g + b


def _proj_kernel(x_ref, w_ref, cos_ref, sin_up_ref, sin_dn_ref, *out_refs):
    xb = x_ref[...].astype(BF16)
    off = 0
    for (_, width, rotary), o_ref in zip(_PROJ_PIECES, out_refs):
        for c0 in range(0, width, _COL_CHUNK):
            cw = min(_COL_CHUNK, width - c0)
            acc = _dot(xb, w_ref[:, off + c0:off + c0 + cw])
            if rotary:
                reps = cw // HEAD_DIM
                cos = jnp.concatenate([cos_ref[...]] * reps, axis=1)
                sin_up = jnp.concatenate([sin_up_ref[...]] * reps, axis=1)
                sin_dn = jnp.concatenate([sin_dn_ref[...]] * reps, axis=1)
                half = ROPE_DIM // 2
                acc = (acc * cos + pltpu.roll(acc, half, 1) * sin_up
                       + pltpu.roll(acc, cw - half, 1) * sin_dn)
            o_ref[:, c0:c0 + cw] = acc.astype(o_ref.dtype)
        off += width


def _rope_tables(seq_len):
    half = ROPE_DIM // 2
    inv_freq = ROPE_THETA ** (-jnp.arange(half, dtype=F32) * 2.0 / ROPE_DIM)
    ang = jnp.arange(seq_len, dtype=F32)[:, None] * inv_freq[None, :]
    cos, sin = jnp.cos(ang), jnp.sin(ang)
    zeros = jnp.zeros((seq_len, HEAD_DIM - ROPE_DIM), F32)
    zh = jnp.zeros((seq_len, half), F32)
    cos_t = jnp.concatenate([cos, cos, zeros + 1.0], axis=1)
    sin_up = jnp.concatenate([zh, sin, zeros], axis=1)
    sin_dn = jnp.concatenate([-sin, zh, zeros], axis=1)
    return cos_t, sin_up, sin_dn


def _project(x2, w_qkv, rope, seq_len, tm=TOKEN_BLOCK):
    n = x2.shape[0]
    blocks_per_seq = seq_len // tm
    tab_spec = pl.BlockSpec((tm, HEAD_DIM), lambda i: (i % blocks_per_seq, 0))
    return pl.pallas_call(
        _proj_kernel,
        grid=(n // tm,),
        in_specs=[pl.BlockSpec((tm, D_MODEL), lambda i: (i, 0)),
                  _resident((D_MODEL, QKV_W)), tab_spec, tab_spec, tab_spec],
        out_specs=[pl.BlockSpec((tm, w), lambda i: (i, 0)) for _, w, _ in _PROJ_PIECES],
        out_shape=[jax.ShapeDtypeStruct((n, w), BF16) for _, w, _ in _PROJ_PIECES],
        compiler_params=_params("parallel"),
        name="qkv_proj",
    )(x2, w_qkv, *rope)


_WIN_CHUNK = 2048
_WIN_SUB = _WIN_CHUNK // WINDOW


def _win_attn_kernel(sink_ref, q_ref, kp_ref, kc_ref, kn_ref, vp_ref, vc_ref, vn_ref, o_ref):
    c = pl.program_id(1)
    rows = A_GROUP * WINDOW
    qi = lax.broadcasted_iota(jnp.int32, (rows, WINDOW), 0) % WINDOW
    kj = lax.broadcasted_iota(jnp.int32, (rows, WINDOW), 1)
    below = kj >= qi
    above = kj <= qi
    prev_pad = jnp.where(c > 0, 0.0, NEG_INF)
    next_pad = jnp.where(c < pl.num_programs(1) - 1, 0.0, NEG_INF)
    for g in range(A_KV_HEADS):
        hs = slice(g * HEAD_DIM, (g + 1) * HEAD_DIM)
        sink = jnp.concatenate(
            [jnp.full((WINDOW, 1), sink_ref[A_GROUP * g + u], F32) for u in range(A_GROUP)], axis=0)
        for sb in range(_WIN_SUB):
            rs = slice(sb * WINDOW, (sb + 1) * WINDOW)
            q = jnp.concatenate(
                [q_ref[rs, (A_GROUP * g + u) * HEAD_DIM:(A_GROUP * g + u + 1) * HEAD_DIM]
                 for u in range(A_GROUP)], axis=0)
            if sb == 0:
                k_prev, v_prev, pad_prev = kp_ref[:, hs], vp_ref[:, hs], prev_pad
            else:
                ps = slice((sb - 1) * WINDOW, sb * WINDOW)
                k_prev, v_prev, pad_prev = kc_ref[ps, hs], vc_ref[ps, hs], 0.0
            if sb == _WIN_SUB - 1:
                k_next, v_next, pad_next = kn_ref[:, hs], vn_ref[:, hs], next_pad
            else:
                ns = slice((sb + 1) * WINDOW, (sb + 2) * WINDOW)
                k_next, v_next, pad_next = kc_ref[ns, hs], vc_ref[ns, hs], 0.0
            s_prev = jnp.where(below, _dot_nt(q, k_prev) * ATTN_SCALE, NEG_INF) + pad_prev
            s_cur = _dot_nt(q, kc_ref[rs, hs]) * ATTN_SCALE
            s_next = jnp.where(above, _dot_nt(q, k_next) * ATTN_SCALE, NEG_INF) + pad_next
            m = jnp.max(jnp.maximum(jnp.maximum(s_prev, s_cur), s_next), axis=-1, keepdims=True)
            m = jnp.maximum(m, sink)
            p_prev, p_cur, p_next = jnp.exp(s_prev - m), jnp.exp(s_cur - m), jnp.exp(s_next - m)
            denom = jnp.sum(p_prev + p_cur + p_next, axis=-1, keepdims=True) + jnp.exp(sink - m)
            o = (_dot(p_prev.astype(BF16), v_prev) + _dot(p_cur.astype(BF16), vc_ref[rs, hs])
                 + _dot(p_next.astype(BF16), v_next)) / denom
            for u in range(A_GROUP):
                h = A_GROUP * g + u
                o_ref[rs, h * HEAD_DIM:(h + 1) * HEAD_DIM] = (
                    o[u * WINDOW:(u + 1) * WINDOW].astype(o_ref.dtype))


def _window_attention(q, k, v, sink, batch, seq_len):
    n = q.shape[0]
    chunks = seq_len // _WIN_CHUNK
    blocks = seq_len // WINDOW

    def cur(b, c):
        return (b * chunks + c, 0)

    def prev(b, c):
        return (b * blocks + jnp.maximum(c * _WIN_SUB - 1, 0), 0)

    def nxt(b, c):
        return (b * blocks + jnp.minimum((c + 1) * _WIN_SUB, blocks - 1), 0)

    def kv_specs():
        return [pl.BlockSpec((WINDOW, A_KV_W), prev), pl.BlockSpec((_WIN_CHUNK, A_KV_W), cur),
                pl.BlockSpec((WINDOW, A_KV_W), nxt)]

    return pl.pallas_call(
        _win_attn_kernel,
        grid=(batch, chunks),
        in_specs=[pl.BlockSpec(memory_space=pltpu.SMEM), pl.BlockSpec((_WIN_CHUNK, A_Q_W), cur),
                  *kv_specs(), *kv_specs()],
        out_specs=pl.BlockSpec((_WIN_CHUNK, A_Q_W), cur),
        out_shape=jax.ShapeDtypeStruct((n, A_Q_W), BF16),
        compiler_params=_params("parallel", "parallel"),
        name="window_attn",
    )(sink, q, k, k, k, v, v, v)


_NA_KEYS = NA_ROWS * GRID_W
_NA_CHUNK = NA_ROWS * GRID_W
_NA_HALO = _NA_CHUNK // 2
_NA_PAIR_KEYS = (NA_ROWS + 2) * GRID_W


def _na_bias_table(rpb):
    c = jnp.arange(GRID_W)
    cs = jnp.clip(c - NA_COLS // 2, 0, GRID_W - NA_COLS)
    col_ok = (c[None, :] >= cs[:, None]) & (c[None, :] < cs[:, None] + NA_COLS)
    dc = jnp.clip(c[None, :] - c[:, None], -(NA_COLS - 1), NA_COLS - 1) + NA_COLS - 1
    onehot = dc[:, :, None] == jnp.arange(2 * NA_COLS - 1)[None, None, :]
    by_col = jnp.sum(jnp.where(onehot[None, None], rpb.astype(F32)[:, :, None, None, :], 0.0), axis=-1)
    by_col = jnp.where(col_ok[None, None], by_col, NEG_INF)
    pats = [by_col[:, NA_ROWS - 1 - p:2 * NA_ROWS - 1 - p] for p in range(NA_ROWS)]
    bias = jnp.stack(pats, axis=0)
    bias = bias.transpose(0, 1, 3, 2, 4).reshape(NA_ROWS, B_HEADS, GRID_W, _NA_KEYS)
    spare = _NA_PAIR_KEYS - _NA_KEYS
    shifted = [jnp.pad(bias, ((0, 0), (0, 0), (0, 0), (s * GRID_W, spare - s * GRID_W)),
                       constant_values=NEG_INF) for s in range(2)]
    return jnp.stack(shifted, axis=1)


def _na_kernel(bias_ref, q_ref, kp_ref, kc_ref, kn_ref, vp_ref, vc_ref, vn_ref, o_ref,
               kbuf, vbuf, *, grid_rows):
    c = pl.program_id(1)
    kbuf[0:_NA_HALO] = kp_ref[...]
    kbuf[_NA_HALO:_NA_HALO + _NA_CHUNK] = kc_ref[...]
    kbuf[_NA_HALO + _NA_CHUNK:] = kn_ref[...]
    vbuf[0:_NA_HALO] = vp_ref[...]
    vbuf[_NA_HALO:_NA_HALO + _NA_CHUNK] = vc_ref[...]
    vbuf[_NA_HALO + _NA_CHUNK:] = vn_ref[...]
    halo_rows = _NA_HALO // GRID_W

    def window_start(r):
        return jnp.clip(r - NA_ROWS // 2, 0, grid_rows - NA_ROWS)

    for i in range(0, NA_ROWS, 2):
        r = c * NA_ROWS + i
        rs0, rs1 = window_start(r), window_start(r + 1)
        start = pl.multiple_of((rs0 - (c * NA_ROWS - halo_rows)) * GRID_W, GRID_W)
        qs = slice(i * GRID_W, (i + 2) * GRID_W)
        for h in range(B_HEADS):
            hs = slice(h * HEAD_DIM, (h + 1) * HEAD_DIM)
            kw = kbuf[pl.ds(start, _NA_PAIR_KEYS), hs]
            vw = vbuf[pl.ds(start, _NA_PAIR_KEYS), hs]
            bias = jnp.concatenate([bias_ref[r - rs0, 0, h], bias_ref[r + 1 - rs1, rs1 - rs0, h]], axis=0)
            s = _dot_nt(q_ref[qs, hs], kw) * ATTN_SCALE + bias
            m = jnp.max(s, axis=-1, keepdims=True)
            p = jnp.exp(s - m)
            denom = jnp.sum(p, axis=-1, keepdims=True)
            o_ref[qs, hs] = (_dot(p.astype(BF16), vw) / denom).astype(o_ref.dtype)


def _neighbourhood_attention(q, k, v, bias, batch, seq_len):
    n = q.shape[0]
    grid_rows = seq_len // GRID_W
    assert grid_rows >= 2 * NA_ROWS and seq_len % _NA_CHUNK == 0
    chunks = seq_len // _NA_CHUNK
    halos = seq_len // _NA_HALO

    def cur(b, c):
        return (b * chunks + c, 0)

    def prev(b, c):
        return (b * halos + jnp.maximum(2 * c - 1, 0), 0)

    def nxt(b, c):
        return (b * halos + jnp.minimum(2 * c + 2, halos - 1), 0)

    def kv_specs():
        return [pl.BlockSpec((_NA_HALO, B_W), prev), pl.BlockSpec((_NA_CHUNK, B_W), cur),
                pl.BlockSpec((_NA_HALO, B_W), nxt)]

    buf = pltpu.VMEM((_NA_CHUNK + 2 * _NA_HALO, B_W), BF16)
    return pl.pallas_call(
        functools.partial(_na_kernel, grid_rows=grid_rows),
        grid=(batch, chunks),
        in_specs=[_resident((NA_ROWS, 2, B_HEADS, GRID_W, _NA_PAIR_KEYS)),
                  pl.BlockSpec((_NA_CHUNK, B_W), cur), *kv_specs(), *kv_specs()],
        out_specs=pl.BlockSpec((_NA_CHUNK, B_W), cur),
        out_shape=jax.ShapeDtypeStruct((n, B_W), BF16),
        scratch_shapes=[buf, buf],
        compiler_params=_params("parallel", "parallel"),
        name="neighbourhood_attn",
    )(bias, q, k, k, k, v, v, v)


def _mem_kv_kernel(mem_ref, w_ref, k_ref, v_ref):
    kv = _dot(mem_ref[...].astype(BF16), w_ref[...])
    k_ref[...] = kv[:, :M_W].astype(k_ref.dtype)
    v_ref[...] = kv[:, M_W:].astype(v_ref.dtype)


def _memory_kv(mem2, w_mem_kv, tm):
    n = mem2.shape[0]
    out = jax.ShapeDtypeStruct((n, M_W), BF16)
    return pl.pallas_call(
        _mem_kv_kernel,
        grid=(n // tm,),
        in_specs=[pl.BlockSpec((tm, D_MODEL), lambda i: (i, 0)), _resident((D_MODEL, 2 * M_W))],
        out_specs=[pl.BlockSpec((tm, M_W), lambda i: (i, 0))] * 2,
        out_shape=[out, out],
        compiler_params=_params("parallel"),
        name="memory_kv",
    )(mem2, w_mem_kv)


def _mem_attn_kernel(q_ref, k_ref, v_ref, o_ref):
    for h in range(M_HEADS):
        hs = slice(h * HEAD_DIM, (h + 1) * HEAD_DIM)
        s = _dot_nt(q_ref[:, hs], k_ref[:, hs]) * ATTN_SCALE
        m = jnp.max(s, axis=-1, keepdims=True)
        p = jnp.exp(s - m)
        denom = jnp.sum(p, axis=-1, keepdims=True)
        o_ref[:, hs] = (_dot(p.astype(BF16), v_ref[:, hs]) / denom).astype(o_ref.dtype)


def _memory_attention(q, km, vm, batch, seq_len, tm=TOKEN_BLOCK):
    n = q.shape[0]
    n_mem = km.shape[0] // batch
    chunks = seq_len // tm
    return pl.pallas_call(
        _mem_attn_kernel,
        grid=(batch, chunks),
        in_specs=[pl.BlockSpec((tm, M_W), lambda b, c: (b * chunks + c, 0)),
                  pl.BlockSpec((n_mem, M_W), lambda b, c: (b, 0)),
                  pl.BlockSpec((n_mem, M_W), lambda b, c: (b, 0))],
        out_specs=pl.BlockSpec((tm, M_W), lambda b, c: (b * chunks + c, 0)),
        out_shape=jax.ShapeDtypeStruct((n, M_W), BF16),
        compiler_params=_params("parallel", "parallel"),
        name="memory_attn",
    )(q, km, vm)


_MERGE_COLS = 512


def _merge_kernel(x_ref, oa_ref, ob_ref, om_ref, wg0_ref, wg1_ref, wg2_ref, bg0_ref, bg1_ref,
                  bg2_ref, wa_ref, wb_ref, wm_ref, wo_ref, g_ref, b_ref, y_ref, yb_ref,
                  xb_ref, acc_ref, *, alpha):
    j = pl.program_id(1)

    @pl.when(j == 0)
    def _():
        xb_ref[...] = x_ref[...].astype(BF16)
        acc_ref[...] = jnp.zeros_like(acc_ref)

    xb = xb_ref[...]
    merged = (jax.nn.sigmoid(_dot(xb, wg0_ref[...]) + bg0_ref[...]) * _dot(oa_ref[...], wa_ref[...])
              + jax.nn.sigmoid(_dot(xb, wg1_ref[...]) + bg1_ref[...]) * _dot(ob_ref[...], wb_ref[...])
              + jax.nn.sigmoid(_dot(xb, wg2_ref[...]) + bg2_ref[...]) * _dot(om_ref[...], wm_ref[...]))
    acc_ref[...] += _dot(merged.astype(BF16), wo_ref[...])

    @pl.when(j == pl.num_programs(1) - 1)
    def _():
        y = _layer_norm(alpha * x_ref[...] + acc_ref[...], g_ref[...], b_ref[...])
        y_ref[...] = y
        yb_ref[...] = y.astype(BF16)


def _merge(x2, o_a, o_b, o_m, w_gate, b_gate, w_pa, w_pb, w_pm, w_out, ln_g, ln_b, alpha,
           tm=TOKEN_BLOCK):
    n = x2.shape[0]
    steps = D_MODEL // _MERGE_COLS
    row = lambda i, j: (i, 0)

    def gate_w(branch):
        return pl.BlockSpec((D_MODEL, _MERGE_COLS), lambda i, j: (0, branch * steps + j))

    def gate_b(branch):
        return pl.BlockSpec((1, _MERGE_COLS), lambda i, j: (0, branch * steps + j))

    col = lambda i, j: (0, j)
    vec = pl.BlockSpec((1, D_MODEL), lambda i, j: (0, 0))

    def once(shape):
        return pl.BlockSpec(shape, row, pipeline_mode=pl.Buffered(1))

    return pl.pallas_call(
        functools.partial(_merge_kernel, alpha=alpha),
        grid=(n // tm, steps),
        in_specs=[pl.BlockSpec((tm, D_MODEL), row), once((tm, A_Q_W)), once((tm, B_W)),
                  once((tm, M_W)),
                  gate_w(0), gate_w(1), gate_w(2), gate_b(0), gate_b(1), gate_b(2),
                  pl.BlockSpec((A_Q_W, _MERGE_COLS), col), pl.BlockSpec((B_W, _MERGE_COLS), col),
                  pl.BlockSpec((M_W, _MERGE_COLS), col),
                  pl.BlockSpec((_MERGE_COLS, D_MODEL), lambda i, j: (j, 0)), vec, vec],
        out_specs=[pl.BlockSpec((tm, D_MODEL), row), pl.BlockSpec((tm, D_MODEL), row)],
        out_shape=[jax.ShapeDtypeStruct((n, D_MODEL), F32), jax.ShapeDtypeStruct((n, D_MODEL), BF16)],
        scratch_shapes=[pltpu.VMEM((tm, D_MODEL), BF16), pltpu.VMEM((tm, D_MODEL), F32)],
        compiler_params=_params("parallel", "arbitrary"),
        name="merge_out_ln",
    )(x2, o_a, o_b, o_m, w_gate, w_gate, w_gate, b_gate, b_gate, b_gate, w_pa, w_pb, w_pm, w_out,
      ln_g, ln_b)


def _take_top(s, count, row_id):
    order = jnp.full(s.shape, float(count), F32)
    vals, ids = [], []
    for a in range(count):
        m = jnp.max(s, axis=0, keepdims=True)
        rid = jnp.min(jnp.where(s == m, row_id, _NO_ROW), axis=0, keepdims=True)
        hit = row_id == rid
        order = jnp.where(hit, float(a), order)
        s = jnp.where(hit, -jnp.inf, s)
        vals.append(m)
        ids.append(rid)
    return vals, ids, order


def _sorting_network(n):
    pairs = []
    p = 1
    while p < n:
        k = p
        while k >= 1:
            for j in range(k % p, n - k, 2 * k):
                for i in range(min(k, n - j - k)):
                    if (i + j) // (2 * p) == (i + j + k) // (2 * p):
                        pairs.append((i + j, i + j + k))
            k //= 2
        p *= 2
    return pairs


def _top_sorted(s):
    assert s.shape[0] == PEER_TOPK * SUBLANES
    v = [s[g * SUBLANES:(g + 1) * SUBLANES] for g in range(PEER_TOPK)]

    def exchange(i, j):
        v[i], v[j] = jnp.maximum(v[i], v[j]), jnp.minimum(v[i], v[j])

    for i, j in _sorting_network(PEER_TOPK):
        exchange(i, j)
    shift = SUBLANES // 2
    while shift >= 1:
        other = [pltpu.roll(x, shift, 0) for x in v]
        v = [jnp.maximum(v[i], other[PEER_TOPK - 1 - i]) for i in range(PEER_TOPK)]
        d = PEER_TOPK // 2
        while d >= 1:
            for i in range(PEER_TOPK):
                if not i & d:
                    exchange(i, i + d)
            d //= 2
        shift //= 2
    return v


def _count_greater(t, x):
    assert len(t) == 16
    b3 = t[7] > x
    b2 = jnp.where(b3, t[11], t[3]) > x
    b1 = jnp.where(b3, jnp.where(b2, t[13], t[9]), jnp.where(b2, t[5], t[1])) > x
    upper = jnp.where(b2, jnp.where(b1, t[14], t[12]), jnp.where(b1, t[10], t[8]))
    lower = jnp.where(b2, jnp.where(b1, t[6], t[4]), jnp.where(b1, t[2], t[0]))
    b0 = jnp.where(b3, upper, lower) > x
    count = (jnp.where(b3, 8.0, 0.0) + jnp.where(b2, 4.0, 0.0) + jnp.where(b1, 2.0, 0.0)
             + jnp.where(b0, 1.0, 0.0))
    return jnp.where(t[15] > x, 16.0, count)


def _top_untied(s):
    t = _top_sorted(s)
    groups = [s[g * SUBLANES:(g + 1) * SUBLANES] for g in range(s.shape[0] // SUBLANES)]
    order = jnp.concatenate([_count_greater(t, x) for x in groups], axis=0)
    tied = jnp.zeros_like(t[0])
    for a in range(PEER_TOPK - 1):
        tied = jnp.maximum(tied, jnp.where(t[a] == t[a + 1], 1.0, 0.0))
    reach = jnp.zeros_like(t[0])
    for x in groups:
        reach = reach + jnp.where(x >= t[PEER_TOPK - 1], 1.0, 0.0)
    shift = SUBLANES // 2
    while shift >= 1:
        reach = reach + pltpu.roll(reach, shift, 0)
        shift //= 2
    tied = jnp.maximum(tied, jnp.where(reach != float(PEER_TOPK), 1.0, 0.0))
    return [x[0:1] for x in t], order, tied


_NO_ROW = 1e9
_CAND_COUNTS = tuple(PEER_TOPK // (a + 1) for a in range(PEER_TOPK))
_CAND_ROWS = -(-sum(_CAND_COUNTS) // SUBLANES) * SUBLANES


_PREP_LANES = 2 * LANES


def _candidate_ids():
    ids = [a * PEER_TOPK + b for a, nb in enumerate(_CAND_COUNTS) for b in range(nb)]
    ids += [_NO_ROW] * (_CAND_ROWS - len(ids))
    return jnp.broadcast_to(jnp.asarray(ids, F32)[:, None], (_CAND_ROWS, _PREP_LANES))


def _score_weights(w_peer_q, keys1, keys2):
    wq = w_peer_q.astype(F32).reshape(D_MODEL, PEER_HEADS, 2, PEER_HALF)
    keys = jnp.stack([keys1, keys2]).astype(F32)
    wk = jnp.einsum('snc,dhsc->hsnd', keys, wq, precision=lax.Precision.HIGHEST)
    return wk.reshape(2 * PEER_HEADS * N_KEYS, D_MODEL).astype(BF16)


def _peer_prep_kernel(xb_ref, wk_ref, cid_ref, n1_ref, e1_ref, r2_ref, e2_ref, st_ref):
    tm = xb_ref.shape[0]
    st_ref[...] = _dot_nt(wk_ref[...], xb_ref[...]).reshape(st_ref.shape)
    width = _PREP_LANES
    lane_chunks = tm // width

    def body(it, carry):
        h = it // lane_chunks
        lanes = pl.ds(pl.multiple_of((it % lane_chunks) * width, width), width)
        s1 = st_ref[2 * h, :, lanes]
        s2 = st_ref[2 * h + 1, :, lanes]
        t1, order1, tied1 = _top_untied(s1)
        t2, order2, tied2 = _top_untied(s2)

        def with_ties():
            key_id = lax.broadcasted_iota(jnp.int32, (N_KEYS, width), 0).astype(F32)
            v1, _, o1 = _take_top(s1, PEER_TOPK, key_id)
            v2, _, o2 = _take_top(s2, PEER_TOPK, key_id)
            return tuple(v1), o1, tuple(v2), o2

        t1, order1, t2, order2 = lax.cond(jnp.max(jnp.maximum(tied1, tied2)) > 0.0, with_ties,
                                          lambda: (tuple(t1), order1, tuple(t2), order2))
        t2_all = jnp.concatenate(t2, axis=0)
        pad = jnp.full((_CAND_ROWS - sum(_CAND_COUNTS), width), -jnp.inf, F32)
        cand = jnp.concatenate(
            [t1[a] + t2_all[:nb] for a, nb in enumerate(_CAND_COUNTS)] + [pad], axis=0)
        top, cid, _ = _take_top(cand, PEER_TOPK, cid_ref[...])
        a_id = lax.broadcasted_iota(jnp.int32, (PEER_TOPK, width), 0).astype(F32)
        taken = jnp.zeros((PEER_TOPK, width), F32)
        z = jnp.zeros((1, width), F32)
        for k in range(PEER_TOPK):
            taken = taken + jnp.where(a_id == jnp.floor(cid[k] * (1.0 / PEER_TOPK)), 1.0, 0.0)
            z = z + jnp.exp(top[k] - top[0])
        n1 = jnp.zeros((N_KEYS, width), F32)
        for a in range(PEER_TOPK):
            n1 = jnp.where(order1 == float(a), taken[a:a + 1, :], n1)
        keys = pl.ds(pl.multiple_of(h * N_KEYS, N_KEYS), N_KEYS)
        n1_ref[keys, lanes] = n1
        e1_ref[keys, lanes] = jnp.exp(s1 - t1[0])
        r2_ref[keys, lanes] = order2.astype(r2_ref.dtype)
        e2_ref[keys, lanes] = (jnp.exp(s2 - t2[0]) / z).astype(e2_ref.dtype)
        return carry

    lax.fori_loop(0, PEER_HEADS * lane_chunks, body, 0)


def _peer_prep(x1b, w_scores, tm=TOKEN_BLOCK):
    n = x1b.shape[0]
    by_key1 = jax.ShapeDtypeStruct((PEER_HEADS * N_KEYS, n), F32)
    by_key2 = jax.ShapeDtypeStruct((PEER_HEADS * N_KEYS, n), BF16)
    spec = pl.BlockSpec((PEER_HEADS * N_KEYS, tm), lambda i: (0, i))
    return pl.pallas_call(
        _peer_prep_kernel,
        grid=(n // tm,),
        in_specs=[pl.BlockSpec((tm, D_MODEL), lambda i: (i, 0)),
                  _resident((2 * PEER_HEADS * N_KEYS, D_MODEL)),
                  _resident((_CAND_ROWS, _PREP_LANES))],
        out_specs=[spec] * 4,
        out_shape=[by_key1, by_key1, by_key2, by_key2],
        scratch_shapes=[pltpu.VMEM((2 * PEER_HEADS, N_KEYS, tm), F32)],
        compiler_params=_params("parallel"),
        name="peer_retrieve",
    )(x1b, w_scores, _candidate_ids())


def _peer_kernel(xb_ref, u_ref, vt_ref, n1_ref, e1_ref, r2_ref, e2_ref, x_ref, g_ref, b_ref,
                 y_ref, acc_ref, a_ref, w_ref, *, alpha):
    j = pl.program_id(1)
    te, tm = a_ref.shape

    @pl.when(j == 0)
    def _():
        acc_ref[...] = jnp.zeros_like(acc_ref)

    a_ref[...] = _dot_nt(u_ref[...], xb_ref[...])
    keys_per_step = te // N_KEYS

    def first_key_row(ref, h, sub, ls):
        return ref[h, :, ls][sub:sub + 1]

    chunk = 2 * LANES

    def packed_rows(row):
        tile_rows = 2 * SUBLANES
        tile = jnp.broadcast_to(row, (tile_rows, chunk)).astype(BF16)
        return jnp.concatenate([tile] * (N_KEYS // tile_rows), axis=0)

    for sub in range(keys_per_step):
        es = slice(sub * N_KEYS, (sub + 1) * N_KEYS)
        for tc in range(tm // chunk):
            ls = slice(tc * chunk, (tc + 1) * chunk)
            gate = jnp.zeros((N_KEYS, chunk), BF16)
            for h in range(PEER_HEADS):
                ks = slice(h * N_KEYS, (h + 1) * N_KEYS)
                n1 = packed_rows(first_key_row(n1_ref, h, sub, ls))
                e1 = packed_rows(first_key_row(e1_ref, h, sub, ls))
                gate = gate + jnp.where(r2_ref[ks, ls] < n1, e2_ref[ks, ls] * e1,
                                        jnp.zeros_like(gate))
            a = a_ref[es, ls]
            gelu = 0.5 * a * (1.0 + lax.erf(a * (2.0 ** -0.5)))
            w_ref[es, ls] = gelu.astype(BF16) * gate

    acc_ref[...] += _dot(vt_ref[...], w_ref[...])

    @pl.when(j == pl.num_programs(1) - 1)
    def _():
        ff = acc_ref[...].T
        y_ref[...] = _layer_norm(alpha * x_ref[...] + ff, g_ref[...], b_ref[...])


def _peer(x1, x1b, u, vt, n1, e1, r2, e2, ln_g, ln_b, alpha, tm=TOKEN_BLOCK, te=PEER_TILE):
    n = x1.shape[0]
    n_experts = u.shape[0]
    assert (te // N_KEYS) % SUBLANES == 0
    row = lambda i, j: (i, 0)
    keys_per_step = te // N_KEYS
    by_key1 = pl.BlockSpec((PEER_HEADS, None, keys_per_step, tm), lambda i, j: (0, j, 0, i))
    by_key2 = pl.BlockSpec((PEER_HEADS * N_KEYS, tm), lambda i, j: (0, i))
    n1, e1 = (a.reshape(PEER_HEADS, N_KEYS // keys_per_step, keys_per_step, n) for a in (n1, e1))
    vec = pl.BlockSpec((1, D_MODEL), lambda i, j: (0, 0))
    return pl.pallas_call(
        functools.partial(_peer_kernel, alpha=alpha),
        grid=(n // tm, n_experts // te),
        in_specs=[pl.BlockSpec((tm, D_MODEL), row),
                  pl.BlockSpec((te, D_MODEL), lambda i, j: (j, 0)),
                  pl.BlockSpec((D_MODEL, te), lambda i, j: (0, j)),
                  by_key1, by_key1, by_key2, by_key2,
                  pl.BlockSpec((tm, D_MODEL), row), vec, vec],
        out_specs=pl.BlockSpec((tm, D_MODEL), row),
        out_shape=jax.ShapeDtypeStruct((n, D_MODEL), F32),
        scratch_shapes=[pltpu.VMEM((D_MODEL, tm), F32), pltpu.VMEM((te, tm), F32),
                        pltpu.VMEM((te, tm), BF16)],
        compiler_params=_params("parallel", "arbitrary"),
        name="peer_experts_ln",
    )(x1b, u, vt, n1, e1, r2, e2, x1, ln_g, ln_b)


def _encoder_layer(x, mem, p, alpha):
    batch, seq_len, _ = x.shape
    x2 = x.reshape(batch * seq_len, D_MODEL)
    q_a, k_a, v_a, q_b, k_b, v_b, q_m = _project(x2, p["w_qkv"], _rope_tables(seq_len), seq_len)
    o_a = _window_attention(q_a, k_a, v_a, p["a_sink"], batch, seq_len)
    o_b = _neighbourhood_attention(q_b, k_b, v_b, p["na_bias"], batch, seq_len)
    km, vm = _memory_kv(mem.reshape(-1, D_MODEL), p["w_mem_kv"], tm=mem.shape[1])
    o_m = _memory_attention(q_m, km, vm, batch, seq_len)
    x1, x1b = _merge(x2, o_a, o_b, o_m, p["w_gate"], p["b_gate"], p["w_proj_a"], p["w_proj_b"],
                     p["w_proj_m"], p["w_out"], p["ln1_g"], p["ln1_b"], alpha)
    n1, e1, r2, e2 = _peer_prep(x1b, p["peer_scores"])
    y = _peer(x1, x1b, p["peer_u"], p["peer_vt"], n1, e1, r2, e2, p["ln2_g"], p["ln2_b"], alpha)
    return y.reshape(batch, seq_len, D_MODEL)


def _layer_params(w_in, b_gate, a_sink, na_rpb, w_mem_kv, w_proj_a, w_proj_b, w_proj_m, w_out,
                  ln1_g, ln1_b, w_peer_q, peer_keys1, peer_keys2, peer_u, peer_v, ln2_g, ln2_b):
    row = lambda v: v.reshape(1, -1).astype(F32)
    return {
        "w_qkv": w_in[:, :QKV_W].astype(BF16),
        "w_gate": w_in[:, QKV_W:].astype(BF16),
        "b_gate": row(b_gate),
        "a_sink": a_sink.astype(F32),
        "na_bias": _na_bias_table(na_rpb),
        "w_mem_kv": w_mem_kv.astype(BF16),
        "w_proj_a": w_proj_a.astype(BF16),
        "w_proj_b": w_proj_b.astype(BF16),
        "w_proj_m": w_proj_m.astype(BF16),
        "w_out": w_out.astype(BF16),
        "ln1_g": row(ln1_g),
        "ln1_b": row(ln1_b),
        "peer_scores": _score_weights(w_peer_q, peer_keys1, peer_keys2),
        "peer_u": peer_u.astype(BF16),
        "peer_vt": peer_v.T.astype(BF16),
        "ln2_g": row(ln2_g),
        "ln2_b": row(ln2_b),
    }


def kernel(x_prompt, x_sample, mem_prompt, mem_sample, w_in, b_gate, a_sink, na_rpb, w_mem_kv,
           w_proj_a, w_proj_b, w_proj_m, w_out, ln1_g, ln1_b, w_peer_q, peer_keys1, peer_keys2,
           peer_u, peer_v, ln2_g, ln2_b):
    depth = w_in.shape[0]
    alpha = (2.0 * depth) ** 0.25
    y_prompt, y_sample = x_prompt, x_sample
    for l in range(depth):
        p = _layer_params(w_in[l], b_gate[l], a_sink[l], na_rpb[l], w_mem_kv[l], w_proj_a[l],
                          w_proj_b[l], w_proj_m[l], w_out[l], ln1_g[l], ln1_b[l], w_peer_q[l],
                          peer_keys1[l], peer_keys2[l], peer_u[l], peer_v[l], ln2_g[l], ln2_b[l])
        y_prompt = _encoder_layer(y_prompt, mem_prompt, p, alpha)
        y_sample = _encoder_layer(y_sample, mem_sample, p, alpha)
    return (y_prompt, y_sample)
```
